```python
import jax, jax.numpy as jnp
from jax import lax
import numpy as np

D_MODEL = 1024
BATCH = 4
SEQ = 8192
DEPTH = 4

CTX_LEN = 256
GRID_W = 64
NORM_EPS = 1e-6
LRU_WIDTH = D_MODEL
LRU_BLOCKS = 16
LRU_BLOCK_W = LRU_WIDTH // LRU_BLOCKS
CONV_WIDTH = 4
RGLRU_C = 8.0
GLA_HEADS = 4
GLA_DK = D_MODEL // 2 // GLA_HEADS
GLA_DV = D_MODEL // GLA_HEADS
GLA_GATE_RANK = 16
GLA_TAU = 16.0
GLA_CHUNK = 64
ATT_HEADS = 8
ATT_KV_HEADS = 2
ATT_GROUP = ATT_HEADS // ATT_KV_HEADS
HEAD_DIM = 128
Q_BLOCK = 128
ROPE_THETA = 10000.0
ROPE_AXIS_DIM = HEAD_DIM // 2
D_FF = 4 * D_MODEL
N_BRANCHES = 3
BRANCH_WIDTH = D_MODEL
PROJ_WIDTHS = (LRU_WIDTH, LRU_WIDTH,
               GLA_HEADS * GLA_DK, GLA_HEADS * GLA_DK, GLA_HEADS * GLA_DV, GLA_HEADS * GLA_DV, 2 * GLA_GATE_RANK,
               ATT_HEADS * HEAD_DIM, ATT_KV_HEADS * HEAD_DIM, ATT_KV_HEADS * HEAD_DIM,
               N_BRANCHES * D_MODEL)
PROJ_SPLITS = tuple(sum(PROJ_WIDTHS[:i + 1]) for i in range(len(PROJ_WIDTHS) - 1))
D_PROJ = sum(PROJ_WIDTHS)

kernel_name = 'hybrid_rglru_gla_gqa_diffusion_trunk'


def rms_norm(x, g):
    xf = x.astype(jnp.float32)
    y = xf * lax.rsqrt(jnp.mean(xf * xf, axis=-1, keepdims=True) + NORM_EPS)
    return (y * g.astype(jnp.float32)).astype(x.dtype)


def modulate(x, g, shift, scale):
    return rms_norm(x, g) * (1 + scale) + shift


def centred_dwconv(x, w, b):
    y = lax.conv_general_dilated(x, w[:, None, :].astype(x.dtype), window_strides=(1,),
                                 padding=[((CONV_WIDTH - 1) // 2, CONV_WIDTH // 2)],
                                 dimension_numbers=('NWC', 'WIO', 'NWC'),
                                 feature_group_count=x.shape[-1])
    return y + b


def linear_scan(a, u, h0, reverse):
    def combine(e1, e2):
        a1, b1 = e1
        a2, b2 = e2
        return a1 * a2, a2 * b1 + b2
    a_cum, b_cum = lax.associative_scan(combine, (a, u), reverse=reverse, axis=1)
    return a_cum * h0[:, None, :] + b_cum


def rglru_coeffs(xc, a_w, a_b, x_w, x_b, lam):
    bn, n, w = xc.shape
    xh = xc.reshape(bn, n, LRU_BLOCKS, LRU_BLOCK_W)
    r = jax.nn.sigmoid(jnp.einsum('bnhi,hij->bnhj', xh, a_w).reshape(bn, n, w) + a_b).astype(jnp.float32)
    i = jax.nn.sigmoid(jnp.einsum('bnhi,hij->bnhj', xh, x_w).reshape(bn, n, w) + x_b)
    log_a = -RGLRU_C * r * jax.nn.softplus(-lam.astype(jnp.float32))
    a = jnp.exp(log_a)
    u = jnp.sqrt(-jnp.expm1(2.0 * log_a)) * (i * xc).astype(jnp.float32)
    return a, u


def rglru_branch(x_lat, y_lat, x_ctx, y_ctx, conv_w, conv_b, a_w, a_b, x_w, x_b, lam, ctx_out):
    xc_lat = centred_dwconv(x_lat, conv_w, conv_b)
    xc_ctx = centred_dwconv(x_ctx, conv_w, conv_b)
    bn, _, w = xc_ctx.shape
    h_lat = []
    h_ctx = []
    for d in range(2):
        rev = d == 1
        a_c, u_c = rglru_coeffs(xc_ctx, a_w[d], a_b[d], x_w[d], x_b[d], lam[d])
        hc = linear_scan(a_c, u_c, jnp.zeros((bn, w), jnp.float32), rev)
        h_end = hc[:, 0] if rev else hc[:, -1]
        a_l, u_l = rglru_coeffs(xc_lat, a_w[d], a_b[d], x_w[d], x_b[d], lam[d])
        h_lat.append(linear_scan(a_l, u_l, h_end, rev))
        h_ctx.append(hc)
    out_lat = ((h_lat[0] + h_lat[1]) * jax.nn.gelu(y_lat.astype(jnp.float32))).astype(x_lat.dtype)
    out_ctx = None
    if ctx_out:
        out_ctx = ((h_ctx[0] + h_ctx[1]) * jax.nn.gelu(y_ctx.astype(jnp.float32))).astype(x_ctx.dtype)
    return out_lat, out_ctx


def gla_chunked(q, k, v, g, s0, with_output):
    bn, n, h, dk = q.shape
    dv = v.shape[-1]
    nc = n // GLA_CHUNK

    def chunks(t):
        return t.reshape(bn, nc, GLA_CHUNK, h, t.shape[-1]).transpose(1, 0, 3, 2, 4)

    qc, kc, vc, gc = chunks(q), chunks(k), chunks(v), chunks(g)
    b = jnp.cumsum(gc, axis=3)
    b_last = b[:, :, :, -1:, :]
    u = jnp.einsum('nbhcd,nbhce->nbhde', kc * jnp.exp(b_last - b), vc)
    decay = jnp.exp(b_last[:, :, :, 0, :])

    def step(s, inp):
        dec, uu = inp
        return dec[..., None] * s + uu, s

    s_final, s_prev = lax.scan(step, s0, (decay, u))
    if not with_output:
        return None, s_final
    b_mid = b[:, :, :, GLA_CHUNK // 2 - 1:GLA_CHUNK // 2, :]
    scores = jnp.einsum('nbhid,nbhjd->nbhij', qc * jnp.exp(b - b_mid), kc * jnp.exp(b_mid - b))
    mask = jnp.tril(jnp.ones((GLA_CHUNK, GLA_CHUNK), dtype=bool))
    scores = jnp.where(mask, scores, 0.0)
    o = (jnp.einsum('nbhij,nbhje->nbhie', scores, vc)
         + jnp.einsum('nbhid,nbhde->nbhie', qc * jnp.exp(b), s_prev))
    return o.transpose(1, 0, 3, 2, 4).reshape(bn, n, h, dv), s_final


def gla_branch(q_lat, k_lat, v_lat, r_lat, z_lat, q_ctx, k_ctx, v_ctx, r_ctx, z_ctx,
               gate_w, gate_b, norm_g, ctx_out):
    def heads(q, k, v):
        bn, n, _ = q.shape
        qh = q.astype(jnp.float32).reshape(bn, n, GLA_HEADS, GLA_DK) * (GLA_DK ** -0.5)
        kh = k.astype(jnp.float32).reshape(bn, n, GLA_HEADS, GLA_DK)
        vh = v.astype(jnp.float32).reshape(bn, n, GLA_HEADS, GLA_DV)
        return qh, kh, vh

    def log_gate(z, d):
        bn, n, _ = z.shape
        zz = z[..., d * GLA_GATE_RANK:(d + 1) * GLA_GATE_RANK] @ gate_w[d] + gate_b[d]
        return (jax.nn.log_sigmoid(zz.astype(jnp.float32)) / GLA_TAU).reshape(bn, n, GLA_HEADS, GLA_DK)

    ql, kl, vl = heads(q_lat, k_lat, v_lat)
    qc, kc, vc = heads(q_ctx, k_ctx, v_ctx)
    bn = q_ctx.shape[0]
    s_zero = jnp.zeros((bn, GLA_HEADS, GLA_DK, GLA_DV), jnp.float32)
    o_lat = []
    o_ctx = []
    for d in range(2):
        gl, gcx = log_gate(z_lat, d), log_gate(z_ctx, d)
        lat_in = (ql, kl, vl, gl)
        ctx_in = (qc, kc, vc, gcx)
        if d == 1:
            lat_in = tuple(jnp.flip(t, axis=1) for t in lat_in)
            ctx_in = tuple(jnp.flip(t, axis=1) for t in ctx_in)
        oc, s_c = gla_chunked(*ctx_in, s_zero, ctx_out)
        ol, _ = gla_chunked(*lat_in, s_c, True)
        if d == 1:
            ol = jnp.flip(ol, axis=1)
            oc = jnp.flip(oc, axis=1) if ctx_out else None
        o_lat.append(ol)
        o_ctx.append(oc)

    def finish(o, r):
        bn_, n_ = o.shape[0], o.shape[1]
        on = rms_norm(o, norm_g).reshape(bn_, n_, GLA_HEADS * GLA_DV)
        return (on * jax.nn.silu(r.astype(jnp.float32))).astype(r.dtype)

    out_lat = finish(o_lat[0] + o_lat[1], r_lat)
    out_ctx = finish(o_ctx[0] + o_ctx[1], r_ctx) if ctx_out else None
    return out_lat, out_ctx


def rope_rotate(x, ang):
    half = x.shape[-1] // 2
    cos = jnp.cos(ang)[None, :, None, :]
    sin = jnp.sin(ang)[None, :, None, :]
    xf = x.astype(jnp.float32)
    x1, x2 = xf[..., :half], xf[..., half:]
    return jnp.concatenate([x1 * cos - x2 * sin, x2 * cos + x1 * sin], axis=-1).astype(x.dtype)


def axial_rope(x, ang_r, ang_c):
    return jnp.concatenate([rope_rotate(x[..., :ROPE_AXIS_DIM], ang_r),
                            rope_rotate(x[..., ROPE_AXIS_DIM:], ang_c)], axis=-1)


def gqa_attend(q, k, v):
    s = jnp.einsum('bqhgd,bkhd->bhgqk', q, k).astype(jnp.float32) * (HEAD_DIM ** -0.5)
    p = jax.nn.softmax(s, axis=-1)
    return jnp.einsum('bhgqk,bkhd->bqhgd', p.astype(v.dtype), v)


def gqa_branch(q_lat, k_lat, v_lat, q_ctx, k_ctx, v_ctx, q_norm_g, k_norm_g, ang_r, ang_c, ctx_out):
    bn, n, _ = q_lat.shape
    l = q_ctx.shape[1]

    def heads(t, h):
        return t.reshape(t.shape[0], t.shape[1], h, HEAD_DIM)

    ql = axial_rope(rms_norm(heads(q_lat, ATT_HEADS), q_norm_g), ang_r, ang_c)
    kl = axial_rope(rms_norm(heads(k_lat, ATT_KV_HEADS), k_norm_g), ang_r, ang_c)
    kc = rms_norm(heads(k_ctx, ATT_KV_HEADS), k_norm_g)
    vc = heads(v_ctx, ATT_KV_HEADS)
    k_all = jnp.concatenate([kc, kl], axis=1)
    v_all = jnp.concatenate([vc, heads(v_lat, ATT_KV_HEADS)], axis=1)
    nb = n // Q_BLOCK
    qb = ql.reshape(bn, nb, Q_BLOCK, ATT_KV_HEADS, ATT_GROUP, HEAD_DIM).transpose(1, 0, 2, 3, 4, 5)
    ob = lax.map(lambda blk: gqa_attend(blk, k_all, v_all), qb)
    out_lat = ob.transpose(1, 0, 2, 3, 4, 5).reshape(bn, n, ATT_HEADS * HEAD_DIM)
    out_ctx = None
    if ctx_out:
        qc = rms_norm(heads(q_ctx, ATT_HEADS), q_norm_g).reshape(bn, l, ATT_KV_HEADS, ATT_GROUP, HEAD_DIM)
        out_ctx = gqa_attend(qc, kc, vc).reshape(bn, l, ATT_HEADS * HEAD_DIM)
    return out_lat, out_ctx


def merge_branches(y_lru, y_gla, y_att, gate_logits, w_branch, w_out):
    g = jax.nn.sigmoid(gate_logits.astype(jnp.float32)).astype(y_lru.dtype)
    g1, g2, g3 = jnp.split(g, N_BRANCHES, axis=-1)
    m = g1 * (y_lru @ w_branch[0]) + g2 * (y_gla @ w_branch[1]) + g3 * (y_att @ w_branch[2])
    return m @ w_out


def squared_relu_mlp(h, w1, w2):
    return jnp.square(jax.nn.relu(h @ w1)) @ w2


def trunk_layer(x, x_ctx, c_silu, cc_silu, mod_w, mod_b, norm1_g, norm2_g, w_in,
                conv_w, conv_b, lru_a_w, lru_a_b, lru_x_w, lru_x_b, lru_lambda,
                gla_gate_w, gla_gate_b, gla_norm_g, q_norm_g, k_norm_g,
                w_branch, w_out, mlp_w1, mlp_w2, ang_r, ang_c, ctx_out):
    sh1, sc1, ga1, sh2, sc2, ga2 = jnp.split((c_silu @ mod_w + mod_b)[:, None, :], 6, axis=-1)
    sh1c, sc1c, ga1c, sh2c, sc2c, ga2c = jnp.split(cc_silu @ mod_w + mod_b, 6, axis=-1)
    h = modulate(x, norm1_g, sh1, sc1)
    hc = modulate(x_ctx, norm1_g, sh1c, sc1c)
    lx, ly, gq, gk, gv, gr, gz, aq, ak, av, bg = jnp.split(h @ w_in, PROJ_SPLITS, axis=-1)
    lxc, lyc, gqc, gkc, gvc, grc, gzc, aqc, akc, avc, bgc = jnp.split(hc @ w_in, PROJ_SPLITS, axis=-1)
    y1, y1c = rglru_branch(lx, ly, lxc, lyc, conv_w, conv_b, lru_a_w, lru_a_b, lru_x_w, lru_x_b,
                           lru_lambda, ctx_out)
    y2, y2c = gla_branch(gq, gk, gv, gr, gz, gqc, gkc, gvc, grc, gzc, gla_gate_w, gla_gate_b,
                         gla_norm_g, ctx_out)
    y3, y3c = gqa_branch(aq, ak, av, aqc, akc, avc, q_norm_g, k_norm_g, ang_r, ang_c, ctx_out)
    x = x + ga1 * merge_branches(y1, y2, y3, bg, w_branch, w_out)
    x = x + ga2 * squared_relu_mlp(modulate(x, norm2_g, sh2, sc2), mlp_w1, mlp_w2)
    if not ctx_out:
        return x, None
    x_ctx = x_ctx + ga1c * merge_branches(y1c, y2c, y3c, bgc, w_branch, w_out)
    x_ctx = x_ctx + ga2c * squared_relu_mlp(modulate(x_ctx, norm2_g, sh2c, sc2c), mlp_w1, mlp_w2)
    return x, x_ctx


def setup_inputs(seed: int = 0) -> dict:
    key = jax.random.key(seed)
    ks = jax.random.split(key, 32)
    f32 = jnp.float32
    nrm = lambda k, shape, s: jax.random.normal(k, shape, f32) * s
    a8 = jax.random.uniform(ks[14], (DEPTH, 2, LRU_WIDTH), f32, minval=0.9, maxval=0.999)
    s_lam = a8 ** (1.0 / RGLRU_C)
    lru_lambda = jnp.log(s_lam) - jnp.log1p(-s_lam)
    return {
        'x': nrm(ks[0], (BATCH, SEQ, D_MODEL), 1.0),
        'c': nrm(ks[1], (BATCH, D_MODEL), 1.0),
        'ctx': nrm(ks[2], (BATCH, CTX_LEN, D_MODEL), 1.0),
        'c_ctx': nrm(ks[3], (D_MODEL,), 1.0),
        'mod_w': nrm(ks[4], (DEPTH, D_MODEL, 6 * D_MODEL), 0.5 * D_MODEL ** -0.5),
        'mod_b': nrm(ks[5], (DEPTH, 6 * D_MODEL), 0.01),
        'norm1_g': 1.0 + nrm(ks[6], (DEPTH, D_MODEL), 0.02),
        'norm2_g': 1.0 + nrm(ks[7], (DEPTH, D_MODEL), 0.02),
        'w_in': nrm(ks[8], (DEPTH, D_MODEL, D_PROJ), D_MODEL ** -0.5),
        'conv_w': nrm(ks[9], (DEPTH, CONV_WIDTH, LRU_WIDTH), CONV_WIDTH ** -0.5),
        'conv_b': nrm(ks[10], (DEPTH, LRU_WIDTH), 0.01),
        'lru_a_w': nrm(ks[11], (DEPTH, 2, LRU_BLOCKS, LRU_BLOCK_W, LRU_BLOCK_W), LRU_BLOCK_W ** -0.5),
        'lru_a_b': nrm(ks[12], (DEPTH, 2, LRU_WIDTH), 0.01),
        'lru_x_w': nrm(ks[13], (DEPTH, 2, LRU_BLOCKS, LRU_BLOCK_W, LRU_BLOCK_W), LRU_BLOCK_W ** -0.5),
        'lru_x_b': nrm(ks[15], (DEPTH, 2, LRU_WIDTH), 0.01),
        'lru_lambda': lru_lambda,
        'gla_gate_w': nrm(ks[16], (DEPTH, 2, GLA_GATE_RANK, GLA_HEADS * GLA_DK), GLA_GATE_RANK ** -0.5),
        'gla_gate_b': nrm(ks[17], (DEPTH, 2, GLA_HEADS * GLA_DK), 0.1),
        'gla_norm_g': 1.0 + nrm(ks[18], (DEPTH, GLA_DV), 0.02),
        'q_norm_g': 1.0 + nrm(ks[19], (DEPTH, HEAD_DIM), 0.02),
        'k_norm_g': 1.0 + nrm(ks[20], (DEPTH, HEAD_DIM), 0.02),
        'w_branch': nrm(ks[21], (DEPTH, N_BRANCHES, BRANCH_WIDTH, D_MODEL), BRANCH_WIDTH ** -0.5),
        'w_out': nrm(ks[22], (DEPTH, D_MODEL, D_MODEL), D_MODEL ** -0.5),
        'mlp_w1': nrm(ks[23], (DEPTH, D_MODEL, D_FF), D_MODEL ** -0.5),
        'mlp_w2': nrm(ks[24], (DEPTH, D_FF, D_MODEL), D_FF ** -0.5),
    }


def reference(x, c, ctx, c_ctx, mod_w, mod_b, norm1_g, norm2_g, w_in, conv_w, conv_b,
              lru_a_w, lru_a_b, lru_x_w, lru_x_b, lru_lambda, gla_gate_w, gla_gate_b, gla_norm_g,
              q_norm_g, k_norm_g, w_branch, w_out, mlp_w1, mlp_w2):
    n_lat = x.shape[1]
    rows = n_lat // GRID_W
    row_pos = jnp.repeat(jnp.arange(rows), GRID_W).astype(jnp.float32)
    col_pos = jnp.tile(jnp.arange(GRID_W), rows).astype(jnp.float32)
    inv_freq = ROPE_THETA ** (-jnp.arange(0, ROPE_AXIS_DIM, 2, dtype=jnp.float32) / ROPE_AXIS_DIM)
    ang_r = row_pos[:, None] * inv_freq[None, :]
    ang_c = col_pos[:, None] * inv_freq[None, :]
    c_silu = jax.nn.silu(c)
    cc_silu = jax.nn.silu(c_ctx)
    x_ctx = ctx
    for l in range(DEPTH):
        x, x_ctx = trunk_layer(x, x_ctx, c_silu, cc_silu, mod_w[l], mod_b[l], norm1_g[l], norm2_g[l], w_in[l],
                               conv_w[l], conv_b[l], lru_a_w[l], lru_a_b[l], lru_x_w[l], lru_x_b[l],
                               lru_lambda[l], gla_gate_w[l], gla_gate_b[l], gla_norm_g[l],
                               q_norm_g[l], k_norm_g[l], w_branch[l], w_out[l], mlp_w1[l], mlp_w2[l],
                               ang_r, ang_c, l < DEPTH - 1)
    return x
```

```python
import functools

import jax
import jax.numpy as jnp
from jax import lax
from jax.experimental import pallas as pl
from jax.experimental.pallas import tpu as pltpu

F32 = jnp.float32
BF16 = jnp.bfloat16

NORM_EPS = 1e-6
GRID_W = 64
ROPE_THETA = 10000.0
RGLRU_C = 8.0
LRU_BLOCK_W = 64
CONV_WIDTH = 4
GLA_HEADS = 4
GLA_DK = 128
GLA_DV = 256
GLA_GATE_RANK = 16
GLA_TAU = 16.0
GLA_CHUNK = 64
ATT_HEADS = 8
ATT_KV_HEADS = 2
ATT_GROUP = ATT_HEADS // ATT_KV_HEADS
HEAD_DIM = 128
N_BRANCHES = 3

LANES = 128
SUBLANES = 8
MXU_DIM = 256
VMEM_LIMIT_BYTES = 56 * 1024 * 1024

SEQ_BLOCK = 256

D = 1024
P_LX, P_LY, P_GV, P_GR, P_AQ = 0, 1024, 2048, 3072, 4096
P_GQ, P_GK, P_BG = 5120, 5632, 6144
P_AK, P_AV, P_GZ = 9216, 9472, 9728
P_GZ_PAD = 256
P_WIDTH = P_GZ + P_GZ_PAD
PROJ_TN = 768


def _first_divisor(n, candidates):
    for c in candidates:
        if n % c == 0:
            return c
    raise ValueError(f"no tile in {candidates} divides {n}")


def _params(*sem):
    return pltpu.CompilerParams(dimension_semantics=sem, vmem_limit_bytes=VMEM_LIMIT_BYTES)


def _rms(x):
    return x * lax.rsqrt(jnp.mean(x * x, axis=-1, keepdims=True) + NORM_EPS)


def _dot(a, b):
    return jnp.dot(a, b, preferred_element_type=F32)


def _dot_nt(a, b):
    return lax.dot_general(a, b, (((1,), (1,)), ((), ())), preferred_element_type=F32)


def _dot_tn(a, b):
    return lax.dot_general(a, b, (((0,), (0,)), ((), ())), preferred_element_type=F32)


def _ctx_rows(tile_idx, tm, ctx_len):
    rows = tile_idx * tm + lax.broadcasted_iota(jnp.int32, (tm, 1), 0)
    return rows < ctx_len


def _mod_kernel(c_ref, w_ref, b_ref, o_ref):
    c = c_ref[...]
    cs = (c * jax.nn.sigmoid(c)).astype(BF16)
    o_ref[...] = _dot(cs, w_ref[...].astype(BF16)) + b_ref[...]


def _modulation(c, c_ctx, mod_w, mod_b):
    n_layers, d, n6 = mod_w.shape
    b = c.shape[0]
    rows = -(-(b + 1) // SUBLANES) * SUBLANES
    cs = jnp.zeros((rows, d), F32).at[:b].set(c).at[b].set(c_ctx)
    tn = _first_divisor(n6, (1536, 1024, 512))
    out = pl.pallas_call(
        _mod_kernel,
        grid=(n_layers, n6 // tn),
        in_specs=[
            pl.BlockSpec((rows, d), lambda l, j: (0, 0)),
            pl.BlockSpec((None, d, tn), lambda l, j: (l, 0, j)),
            pl.BlockSpec((None, 1, tn), lambda l, j: (l, 0, j)),
        ],
        out_specs=pl.BlockSpec((None, rows, tn), lambda l, j: (l, 0, j)),
        out_shape=jax.ShapeDtypeStruct((n_layers, rows, n6), F32),
        compiler_params=_params("parallel", "parallel"),
        name="modulation",
    )(cs, mod_w, mod_b.reshape(n_layers, 1, n6))
    return out.reshape(n_layers, rows, 6, d)[:, :b + 1]


def _proj_kernel(x_ref, modb_ref, modc_ref, g_ref, w_ref, o_ref, h_ref, *, tm, ctx_len):
    i = pl.program_id(1)

    @pl.when(pl.program_id(2) == 0)
    def _():
        y = _rms(x_ref[...]) * g_ref[...]
        is_ctx = _ctx_rows(i, tm, ctx_len)
        shift = jnp.where(is_ctx, modc_ref[0:1, :], modb_ref[0:1, :])
        scale = jnp.where(is_ctx, modc_ref[1:2, :], modb_ref[1:2, :])
        h_ref[...] = (y * (1.0 + scale) + shift).astype(BF16)

    o_ref[...] = _dot(h_ref[...], w_ref[...])


def _proj(xs, mods, norm_g, wp, ctx_len):
    b, s, d = xs.shape
    tm = _first_divisor(s, (1408, 1280, 640, 256))
    nb = mods.shape[0] - 1
    return pl.pallas_call(
        functools.partial(_proj_kernel, tm=tm, ctx_len=ctx_len),
        grid=(b, s // tm, P_WIDTH // PROJ_TN),
        in_specs=[
            pl.BlockSpec((None, tm, d), lambda bi, i, j: (bi, i, 0)),
            pl.BlockSpec((None, 6, d), lambda bi, i, j: (bi, 0, 0)),
            pl.BlockSpec((None, 6, d), lambda bi, i, j: (nb, 0, 0)),
            pl.BlockSpec((1, d), lambda bi, i, j: (0, 0)),
            pl.BlockSpec((d, PROJ_TN), lambda bi, i, j: (0, j)),
        ],
        out_specs=pl.BlockSpec((None, tm, PROJ_TN), lambda bi, i, j: (bi, i, j)),
        out_shape=jax.ShapeDtypeStruct((b, s, P_WIDTH), F32),
        scratch_shapes=[pltpu.VMEM((tm, d), BF16)],
        compiler_params=_params("parallel", "parallel", "arbitrary"),
        name="proj",
    )(xs, mods, mods, norm_g.reshape(1, d), wp)


def _seq_block(tb, nblk, rev):
    if not rev:
        return tb
    return jnp.where(tb == 0, 0, nblk - tb)


def _gelu_tanh(x):
    return 0.5 * x * (1.0 + jnp.tanh(0.7978845608028654 * (x + 0.044715 * (x * x * x))))


def _lru_kernel(*refs, rev, nblk):
    if rev:
        (lx_ref, prev_ref, next_ref, cw_ref, cb_ref, wa_ref, ab_ref, wx_ref, xb_ref, lam_ref,
         hf_ref, ly_ref, o_ref, carry_ref, xpad_ref) = refs
    else:
        (lx_ref, prev_ref, next_ref, cw_ref, cb_ref, wa_ref, ab_ref, wx_ref, xb_ref, lam_ref,
         o_ref, carry_ref, xpad_ref) = refs
    tb = pl.program_id(1)
    te = _seq_block(tb, nblk, rev)
    r, w = lx_ref.shape

    @pl.when(tb == 0)
    def _():
        carry_ref[...] = jnp.zeros_like(carry_ref)

    x = lx_ref[...]
    has_prev = te >= 2
    has_next = jnp.logical_and(te >= 1, te <= nblk - 2)
    xpad_ref[0:SUBLANES, :] = jnp.where(has_prev, prev_ref[...], 0.0)
    xpad_ref[SUBLANES:SUBLANES + r, :] = x
    xpad_ref[SUBLANES + r:2 * SUBLANES + r, :] = jnp.where(has_next, next_ref[...], 0.0)
    xc = (cw_ref[0:1, :] * xpad_ref[SUBLANES - 1:SUBLANES - 1 + r, :]
          + cw_ref[1:2, :] * x
          + cw_ref[2:3, :] * xpad_ref[SUBLANES + 1:SUBLANES + 1 + r, :]
          + cw_ref[3:4, :] * xpad_ref[SUBLANES + 2:SUBLANES + 2 + r, :]
          + cb_ref[...])

    xcb = xc.astype(BF16)
    lam = lam_ref[...]
    neg_sp = -(jnp.maximum(-lam, 0.0) + jnp.log1p(jnp.exp(-jnp.abs(lam))))
    a_parts, u_parts = [], []
    for gi in range(w // MXU_DIM):
        sl = slice(gi * MXU_DIM, (gi + 1) * MXU_DIM)
        rg = jax.nn.sigmoid(_dot(xcb[:, sl], wa_ref[gi]) + ab_ref[:, sl])
        ig = jax.nn.sigmoid(_dot(xcb[:, sl], wx_ref[gi]) + xb_ref[:, sl])
        log_a = RGLRU_C * rg * neg_sp[:, sl]
        a_g = jnp.exp(log_a)
        a_parts.append(a_g)
        one_minus_a2 = -jnp.tanh(log_a) * (1.0 + a_g * a_g)
        u_parts.append(jnp.sqrt(one_minus_a2) * (ig * xc[:, sl]))
    a = jnp.concatenate(a_parts, axis=1)
    u = jnp.concatenate(u_parts, axis=1)

    row = lax.broadcasted_iota(jnp.int32, (SUBLANES, w), 0)
    carry = carry_ref[...]
    groups = range(r // SUBLANES)
    for g in (reversed(groups) if rev else groups):
        rows = slice(g * SUBLANES, (g + 1) * SUBLANES)
        ag, ug = a[rows], u[rows]
        for k in (1, 2, 4):
            if rev:
                shift, valid = SUBLANES - k, row < SUBLANES - k
            else:
                shift, valid = k, row >= k
            a_sh = pltpu.roll(ag, shift, 0)
            u_sh = pltpu.roll(ug, shift, 0)
            ug = ug + ag * jnp.where(valid, u_sh, 0.0)
            ag = ag * jnp.where(valid, a_sh, 1.0)
        h = ag * carry + ug
        if rev:
            o_ref[rows, :] = ((hf_ref[rows, :] + h) * _gelu_tanh(ly_ref[rows, :])).astype(o_ref.dtype)
            carry = h[0:1, :]
        else:
            o_ref[rows, :] = h
            carry = h[SUBLANES - 1:SUBLANES, :]
    carry_ref[...] = carry


def _lru(p, conv_w, conv_b, wa, ab, wx, xb, lam, rev, hf=None):
    b, s, _ = p.shape
    w = conv_w.shape[1]
    r = SEQ_BLOCK
    nblk = s // r
    per8 = r // SUBLANES
    last8 = s // SUBLANES - 1
    blk = lambda tb: _seq_block(tb, nblk, rev)
    ng = w // MXU_DIM
    in_specs = [
        pl.BlockSpec((None, r, w), lambda bi, tb: (bi, blk(tb), P_LX // w)),
        pl.BlockSpec((None, SUBLANES, w), lambda bi, tb: (bi, jnp.maximum(blk(tb) * per8 - 1, 0), P_LX // w)),
        pl.BlockSpec((None, SUBLANES, w), lambda bi, tb: (bi, jnp.minimum((blk(tb) + 1) * per8, last8), P_LX // w)),
        pl.BlockSpec((CONV_WIDTH, w), lambda bi, tb: (0, 0)),
        pl.BlockSpec((1, w), lambda bi, tb: (0, 0)),
        pl.BlockSpec((ng, MXU_DIM, MXU_DIM), lambda bi, tb: (0, 0, 0)),
        pl.BlockSpec((1, w), lambda bi, tb: (0, 0)),
        pl.BlockSpec((ng, MXU_DIM, MXU_DIM), lambda bi, tb: (0, 0, 0)),
        pl.BlockSpec((1, w), lambda bi, tb: (0, 0)),
        pl.BlockSpec((1, w), lambda bi, tb: (0, 0)),
    ]
    args = [p, p, p, conv_w, conv_b.reshape(1, w), wa, ab.reshape(1, w), wx, xb.reshape(1, w),
            lam.reshape(1, w)]
    if rev:
        in_specs += [
            pl.BlockSpec((None, r, w), lambda bi, tb: (bi, blk(tb), 0)),
            pl.BlockSpec((None, r, w), lambda bi, tb: (bi, blk(tb), P_LY // w)),
        ]
        args += [hf, p]
    return pl.pallas_call(
        functools.partial(_lru_kernel, rev=rev, nblk=nblk),
        grid=(b, nblk),
        in_specs=in_specs,
        out_specs=pl.BlockSpec((None, r, w), lambda bi, tb: (bi, blk(tb), 0)),
        out_shape=jax.ShapeDtypeStruct((b, s, w), BF16 if rev else F32),
        scratch_shapes=[pltpu.VMEM((1, w), F32), pltpu.VMEM((r + 2 * SUBLANES, w), F32)],
        compiler_params=_params("arbitrary", "arbitrary"),
        name="lru_rev" if rev else "lru_fwd",
    )(*args)


def _split3(x):
    hi = x.astype(BF16)
    r1 = x - hi.astype(F32)
    mid = r1.astype(BF16)
    lo = (r1 - mid.astype(F32)).astype(BF16)
    return hi, mid, lo


def _gla_kernel(*refs, rev):
    if rev:
        q_ref, k_ref, v_ref, z_ref, wg_ref, gb_ref, of_ref, r_ref, ng_ref, o_ref, st_ref = refs
    else:
        q_ref, k_ref, v_ref, z_ref, wg_ref, gb_ref, o_ref, st_ref = refs
    r = q_ref.shape[0]
    c = GLA_CHUNK
    nch = r // c

    @pl.when(pl.program_id(1) == 0)
    def _():
        st_ref[...] = jnp.zeros_like(st_ref)

    zz = _dot(z_ref[...].astype(BF16), wg_ref[...]) + gb_ref[...]
    g = (jnp.minimum(zz, 0.0) - jnp.log1p(jnp.exp(-jnp.abs(zz)))) * (1.0 / GLA_TAU)

    ri = lax.broadcasted_iota(jnp.int32, (r, r), 0)
    ci = lax.broadcasted_iota(jnp.int32, (r, r), 1)
    same = lax.shift_right_logical(ri, 6) == lax.shift_right_logical(ci, 6)
    tri = jnp.logical_and(same, (ci >= ri) if rev else (ci <= ri))
    tri_b = jnp.where(tri, 1.0, 0.0).astype(BF16)
    bcum = sum(_dot(tri_b, part) for part in _split3(g))

    q = q_ref[...] * (GLA_DK ** -0.5)
    k = k_ref[...]
    qd, kd, kl, qe, dec = [], [], [], [], []
    for ch in range(nch):
        rows = slice(ch * c, (ch + 1) * c)
        b_c = bcum[rows]
        if rev:
            last, mid = b_c[0:1], b_c[c // 2:c // 2 + 1]
        else:
            last, mid = b_c[c - 1:c], b_c[c // 2 - 1:c // 2]
        qd.append(q[rows] * jnp.exp(b_c - mid))
        kd.append(k[rows] * jnp.exp(mid - b_c))
        kl.append(k[rows] * jnp.exp(last - b_c))
        qe.append(q[rows] * jnp.exp(b_c))
        dec.append(jnp.exp(last))
    qd = jnp.concatenate(qd, axis=0).astype(BF16)
    kd = jnp.concatenate(kd, axis=0).astype(BF16)
    kl = jnp.concatenate(kl, axis=0).astype(BF16)
    qe = jnp.concatenate(qe, axis=0).astype(BF16)

    chunks = range(nch)
    for h in range(GLA_HEADS):
        hl = slice(h * GLA_DK, (h + 1) * GLA_DK)
        vl = slice(h * GLA_DV, (h + 1) * GLA_DV)
        vb = v_ref[:, vl].astype(BF16)
        sc = jnp.where(tri, _dot_nt(qd[:, hl], kd[:, hl]), 0.0)
        o_h = _dot(sc.astype(BF16), vb)
        st = st_ref[h]
        o_parts = [None] * nch
        for ch in (reversed(chunks) if rev else chunks):
            rows = slice(ch * c, (ch + 1) * c)
            o_parts[ch] = o_h[rows] + _dot_nt(qe[rows, hl], st.astype(BF16))
            st = st * dec[ch][:, hl] + _dot_tn(vb[rows], kl[rows, hl])
        st_ref[h] = st
        o = jnp.concatenate(o_parts, axis=0)
        if rev:
            o = _rms(of_ref[:, vl] + o) * ng_ref[...]
            rg = r_ref[:, vl]
            o = o * (rg * jax.nn.sigmoid(rg))
        o_ref[:, vl] = o.astype(o_ref.dtype)


def _gla(p, wg, gb, rev, of=None, norm_g=None):
    b, s, _ = p.shape
    r = SEQ_BLOCK
    nblk = s // r
    dk = GLA_HEADS * GLA_DK
    dv = GLA_HEADS * GLA_DV
    blk = lambda tb: _seq_block(tb, nblk, rev)
    in_specs = [
        pl.BlockSpec((None, r, dk), lambda bi, tb: (bi, blk(tb), P_GQ // dk)),
        pl.BlockSpec((None, r, dk), lambda bi, tb: (bi, blk(tb), P_GK // dk)),
        pl.BlockSpec((None, r, dv), lambda bi, tb: (bi, blk(tb), P_GV // dv)),
        pl.BlockSpec((None, r, LANES), lambda bi, tb: (bi, blk(tb), P_GZ // LANES)),
        pl.BlockSpec((LANES, dk), lambda bi, tb: (0, 0)),
        pl.BlockSpec((1, dk), lambda bi, tb: (0, 0)),
    ]
    args = [p, p, p, p, wg, gb.reshape(1, dk)]
    if rev:
        in_specs += [
            pl.BlockSpec((None, r, dv), lambda bi, tb: (bi, blk(tb), 0)),
            pl.BlockSpec((None, r, dv), lambda bi, tb: (bi, blk(tb), P_GR // dv)),
            pl.BlockSpec((1, GLA_DV), lambda bi, tb: (0, 0)),
        ]
        args += [of, p, norm_g.reshape(1, GLA_DV)]
    return pl.pallas_call(
        functools.partial(_gla_kernel, rev=rev),
        grid=(b, nblk),
        in_specs=in_specs,
        out_specs=pl.BlockSpec((None, r, dv), lambda bi, tb: (bi, blk(tb), 0)),
        out_shape=jax.ShapeDtypeStruct((b, s, dv), BF16 if rev else F32),
        scratch_shapes=[pltpu.VMEM((GLA_HEADS, GLA_DV, GLA_DK), F32)],
        compiler_params=_params("arbitrary", "arbitrary"),
        name="gla_rev" if rev else "gla_fwd",
    )(*args)


def _rope_tables(n_lat, ctx_len):
    half = HEAD_DIM // 4
    t = jnp.arange(n_lat)
    inv_freq = ROPE_THETA ** (-jnp.arange(0, 2 * half, 2, dtype=F32) / (2 * half))
    ang_r = (t // GRID_W).astype(F32)[:, None] * inv_freq[None, :]
    ang_c = (t % GRID_W).astype(F32)[:, None] * inv_freq[None, :]
    cos = jnp.concatenate([jnp.cos(ang_r)] * 2 + [jnp.cos(ang_c)] * 2, axis=-1)
    sin = jnp.concatenate([-jnp.sin(ang_r), jnp.sin(ang_r), -jnp.sin(ang_c), jnp.sin(ang_c)], axis=-1)
    cos = jnp.concatenate([jnp.ones((ctx_len, HEAD_DIM), F32), cos], axis=0)
    sin = jnp.concatenate([jnp.zeros((ctx_len, HEAD_DIM), F32), sin], axis=0)
    return cos, sin


def _qkv_kernel(aq_ref, ak_ref, av_ref, cos_ref, sin_ref, qg_ref, kg_ref, q_ref, k_ref, v_ref):
    tm = aq_ref.shape[0]
    cos, sin = cos_ref[...], sin_ref[...]
    lane = lax.broadcasted_iota(jnp.int32, (tm, HEAD_DIM), 1)
    first = (lane & (HEAD_DIM // 2 - 1)) < HEAD_DIM // 4

    def norm_rope(x, g):
        y = _rms(x) * g
        partner = jnp.where(first, pltpu.roll(y, HEAD_DIM - HEAD_DIM // 4, 1), pltpu.roll(y, HEAD_DIM // 4, 1))
        return y * cos + partner * sin

    for h in range(ATT_HEADS):
        sl = slice(h * HEAD_DIM, (h + 1) * HEAD_DIM)
        q_ref[:, sl] = (norm_rope(aq_ref[:, sl], qg_ref[...]) * (HEAD_DIM ** -0.5)).astype(BF16)
    for h in range(ATT_KV_HEADS):
        sl = slice(h * HEAD_DIM, (h + 1) * HEAD_DIM)
        k_ref[:, sl] = norm_rope(ak_ref[:, sl], kg_ref[...]).astype(BF16)
        v_ref[:, 2 * h * HEAD_DIM:(2 * h + 1) * HEAD_DIM] = av_ref[:, sl].astype(BF16)
        v_ref[:, (2 * h + 1) * HEAD_DIM:(2 * h + 2) * HEAD_DIM] = jnp.ones((tm, HEAD_DIM), BF16)


def _qkv(p, cos, sin, q_norm_g, k_norm_g):
    b, s, _ = p.shape
    tm = _first_divisor(s, (768, 256))
    dq = ATT_HEADS * HEAD_DIM
    dkv = ATT_KV_HEADS * HEAD_DIM
    return pl.pallas_call(
        _qkv_kernel,
        grid=(b, s // tm),
        in_specs=[
            pl.BlockSpec((None, tm, dq), lambda bi, i: (bi, i, P_AQ // dq)),
            pl.BlockSpec((None, tm, dkv), lambda bi, i: (bi, i, P_AK // dkv)),
            pl.BlockSpec((None, tm, dkv), lambda bi, i: (bi, i, P_AV // dkv)),
            pl.BlockSpec((tm, HEAD_DIM), lambda bi, i: (i, 0)),
            pl.BlockSpec((tm, HEAD_DIM), lambda bi, i: (i, 0)),
            pl.BlockSpec((1, HEAD_DIM), lambda bi, i: (0, 0)),
            pl.BlockSpec((1, HEAD_DIM), lambda bi, i: (0, 0)),
        ],
        out_specs=[
            pl.BlockSpec((None, tm, dq), lambda bi, i: (bi, i, 0)),
            pl.BlockSpec((None, tm, dkv), lambda bi, i: (bi, i, 0)),
            pl.BlockSpec((None, tm, 2 * dkv), lambda bi, i: (bi, i, 0)),
        ],
        out_shape=[
            jax.ShapeDtypeStruct((b, s, dq), BF16),
            jax.ShapeDtypeStruct((b, s, dkv), BF16),
            jax.ShapeDtypeStruct((b, s, 2 * dkv), BF16),
        ],
        compiler_params=_params("parallel", "parallel"),
        name="qkv",
    )(p, p, p, cos, sin, q_norm_g.reshape(1, HEAD_DIM), k_norm_g.reshape(1, HEAD_DIM))


def _attn_kernel(q_ref, k_ref, v_ref, o_ref, qs_ref, m_ref, acc_ref, *, ck, n_ctx_chunks):
    tq = q_ref.shape[0]
    nck = k_ref.shape[0] // ck
    for h in range(ATT_GROUP):
        qs_ref[h * tq:(h + 1) * tq, :] = q_ref[:, h * HEAD_DIM:(h + 1) * HEAD_DIM]
    m_ref[...] = jnp.full_like(m_ref, -jnp.inf)
    acc_ref[...] = jnp.zeros_like(acc_ref)

    def body(c, carry):
        start = pl.multiple_of(c * ck, ck)
        s = _dot_nt(qs_ref[...], k_ref[pl.ds(start, ck), :])
        m_prev = m_ref[...]
        m_new = jnp.maximum(m_prev, jnp.max(s, axis=-1, keepdims=True))
        p = jnp.exp(s - m_new)
        acc_ref[...] = jnp.exp(m_prev - m_new) * acc_ref[...] + _dot(p.astype(BF16), v_ref[pl.ds(start, ck), :])
        m_ref[...] = m_new
        return carry

    n = jnp.where(pl.program_id(2) == 0, n_ctx_chunks, nck)
    lax.fori_loop(0, n, body, 0)
    acc = acc_ref[...]
    out = acc[:, :HEAD_DIM] / acc[:, HEAD_DIM:]
    for h in range(ATT_GROUP):
        o_ref[:, h * HEAD_DIM:(h + 1) * HEAD_DIM] = out[h * tq:(h + 1) * tq].astype(o_ref.dtype)


def _attn(qn, kn, vn, ctx_len):
    b, s, dq = qn.shape
    tq = ck = SEQ_BLOCK
    gw = ATT_GROUP * HEAD_DIM
    return pl.pallas_call(
        functools.partial(_attn_kernel, ck=ck, n_ctx_chunks=ctx_len // ck),
        grid=(b, ATT_KV_HEADS, s // tq),
        in_specs=[
            pl.BlockSpec((None, tq, gw), lambda bi, h, i: (bi, i, h)),
            pl.BlockSpec((None, s, HEAD_DIM), lambda bi, h, i: (bi, 0, h)),
            pl.BlockSpec((None, s, 2 * HEAD_DIM), lambda bi, h, i: (bi, 0, h)),
        ],
        out_specs=pl.BlockSpec((None, tq, gw), lambda bi, h, i: (bi, i, h)),
        out_shape=jax.ShapeDtypeStruct((b, s, dq), BF16),
        scratch_shapes=[
            pltpu.VMEM((ATT_GROUP * tq, HEAD_DIM), BF16),
            pltpu.VMEM((ATT_GROUP * tq, 1), F32),
            pltpu.VMEM((ATT_GROUP * tq, 2 * HEAD_DIM), F32),
        ],
        compiler_params=_params("parallel", "parallel", "arbitrary"),
        name="attn",
    )(qn, kn, vn)


def _merge_kernel(x_ref, y1_ref, y2_ref, y3_ref, bg_ref, modb_ref, modc_ref, wb_ref, wo_ref, o_ref,
                  *, tm, ctx_len):
    d = x_ref.shape[1]
    gate = jax.nn.sigmoid(bg_ref[...])
    m = (gate[:, 0:d] * _dot(y1_ref[...], wb_ref[0])
         + gate[:, d:2 * d] * _dot(y2_ref[...], wb_ref[1])
         + gate[:, 2 * d:3 * d] * _dot(y3_ref[...], wb_ref[2]))
    upd = _dot(m.astype(BF16), wo_ref[...])
    is_ctx = _ctx_rows(pl.program_id(1), tm, ctx_len)
    ga = jnp.where(is_ctx, modc_ref[2:3, :], modb_ref[2:3, :])
    o_ref[...] = x_ref[...] + ga * upd


def _merge(xs, y1, y2, y3, p, mods, wb, wo, ctx_len):
    b, s, d = xs.shape
    tm = _first_divisor(s, (384, 256))
    nb = mods.shape[0] - 1
    row = lambda bi, i: (bi, i, 0)
    return pl.pallas_call(
        functools.partial(_merge_kernel, tm=tm, ctx_len=ctx_len),
        grid=(b, s // tm),
        in_specs=[
            pl.BlockSpec((None, tm, d), row),
            pl.BlockSpec((None, tm, d), row),
            pl.BlockSpec((None, tm, d), row),
            pl.BlockSpec((None, tm, d), row),
            pl.BlockSpec((None, tm, N_BRANCHES * d), lambda bi, i: (bi, i, P_BG // (N_BRANCHES * d))),
            pl.BlockSpec((None, 6, d), lambda bi, i: (bi, 0, 0)),
            pl.BlockSpec((None, 6, d), lambda bi, i: (nb, 0, 0)),
            pl.BlockSpec((N_BRANCHES, d, d), lambda bi, i: (0, 0, 0)),
            pl.BlockSpec((d, d), lambda bi, i: (0, 0)),
        ],
        out_specs=pl.BlockSpec((None, tm, d), row),
        out_shape=jax.ShapeDtypeStruct((b, s, d), F32),
        compiler_params=_params("parallel", "parallel"),
        name="merge",
    )(xs, y1, y2, y3, p, mods, mods, wb, wo)


def _mlp_kernel(x_ref, modb_ref, modc_ref, g_ref, w1_ref, w2_ref, o_ref, h_ref, acc_ref, *, tm, ctx_len):
    i = pl.program_id(1)
    j = pl.program_id(2)
    is_ctx = _ctx_rows(i, tm, ctx_len)

    @pl.when(j == 0)
    def _():
        y = _rms(x_ref[...]) * g_ref[...]
        shift = jnp.where(is_ctx, modc_ref[3:4, :], modb_ref[3:4, :])
        scale = jnp.where(is_ctx, modc_ref[4:5, :], modb_ref[4:5, :])
        h_ref[...] = (y * (1.0 + scale) + shift).astype(BF16)
        acc_ref[...] = jnp.zeros_like(acc_ref)

    t = jnp.maximum(_dot(h_ref[...], w1_ref[...]), 0.0)
    acc_ref[...] += _dot((t * t).astype(BF16), w2_ref[...])

    @pl.when(j == pl.num_programs(2) - 1)
    def _():
        ga = jnp.where(is_ctx, modc_ref[5:6, :], modb_ref[5:6, :])
        o_ref[...] = x_ref[...] + ga * acc_ref[...]


def _mlp(xs, mods, norm_g, w1, w2, ctx_len):
    b, s, d = xs.shape
    dff = w1.shape[1]
    tm = _first_divisor(s, (1408, 1280, 640, 256))
    tf = _first_divisor(dff, (1024, 512))
    nb = mods.shape[0] - 1
    return pl.pallas_call(
        functools.partial(_mlp_kernel, tm=tm, ctx_len=ctx_len),
        grid=(b, s // tm, dff // tf),
        in_specs=[
            pl.BlockSpec((None, tm, d), lambda bi, i, j: (bi, i, 0)),
            pl.BlockSpec((None, 6, d), lambda bi, i, j: (bi, 0, 0)),
            pl.BlockSpec((None, 6, d), lambda bi, i, j: (nb, 0, 0)),
            pl.BlockSpec((1, d), lambda bi, i, j: (0, 0)),
            pl.BlockSpec((d, tf), lambda bi, i, j: (0, j)),
            pl.BlockSpec((tf, d), lambda bi, i, j: (j, 0)),
        ],
        out_specs=pl.BlockSpec((None, tm, d), lambda bi, i, j: (bi, i, 0)),
        out_shape=jax.ShapeDtypeStruct((b, s, d), F32),
        scratch_shapes=[pltpu.VMEM((tm, d), BF16), pltpu.VMEM((tm, d), F32)],
        compiler_params=_params("parallel", "parallel", "arbitrary"),
        name="mlp",
    )(xs, mods, mods, norm_g.reshape(1, d), w1, w2)


def _pack_w_in(w_in):
    lx, ly, gq, gk, gv, gr, gz, aq, ak, av, bg = jnp.split(
        w_in, (1024, 2048, 2560, 3072, 4096, 5120, 5152, 6176, 6432, 6688), axis=-1)
    pad = jnp.zeros(w_in.shape[:-1] + (P_GZ_PAD - gz.shape[-1],), w_in.dtype)
    return jnp.concatenate([lx, ly, gv, gr, aq, gq, gk, bg, ak, av, gz, pad], axis=-1).astype(BF16)


def _block_diag(w):
    per = MXU_DIM // LRU_BLOCK_W
    lead = w.shape[:-3]
    wg = w.reshape(lead + (w.shape[-3] // per, per, LRU_BLOCK_W, LRU_BLOCK_W))
    eye = jnp.eye(per, dtype=w.dtype)
    bd = jnp.einsum("...gaij,ac->...gaicj", wg, eye)
    return bd.reshape(lead + (w.shape[-3] // per, MXU_DIM, MXU_DIM)).astype(BF16)


def _pad_gate_w(gate_w):
    n_layers, n_dir, rank, dk = gate_w.shape
    out = jnp.zeros((n_layers, n_dir, LANES, dk), gate_w.dtype)
    for d in range(n_dir):
        out = out.at[:, d, d * rank:(d + 1) * rank].set(gate_w[:, d])
    return out.astype(BF16)


def kernel(x, c, ctx, c_ctx, mod_w, mod_b, norm1_g, norm2_g, w_in, conv_w, conv_b, lru_a_w, lru_a_b,
           lru_x_w, lru_x_b, lru_lambda, gla_gate_w, gla_gate_b, gla_norm_g, q_norm_g, k_norm_g,
           w_branch, w_out, mlp_w1, mlp_w2):
    n_lat = x.shape[1]
    ctx_len = ctx.shape[1]
    assert x.shape[2] == D and ctx_len == SEQ_BLOCK and n_lat % SEQ_BLOCK == 0
    n_layers = mod_w.shape[0]

    xs = jnp.concatenate([ctx, x], axis=1)
    mods = _modulation(c, c_ctx, mod_w, mod_b)
    cos, sin = _rope_tables(n_lat, ctx_len)
    wp = _pack_w_in(w_in)
    wa = _block_diag(lru_a_w)
    wx = _block_diag(lru_x_w)
    wg = _pad_gate_w(gla_gate_w)
    wb = w_branch.astype(BF16)
    wo = w_out.astype(BF16)
    w1 = mlp_w1.astype(BF16)
    w2 = mlp_w2.astype(BF16)

    for l in range(n_layers):
        p = _proj(xs, mods[l], norm1_g[l], wp[l], ctx_len)
        lru = lambda dr, **kw: _lru(p, conv_w[l], conv_b[l], wa[l, dr], lru_a_b[l, dr], wx[l, dr],
                                    lru_x_b[l, dr], lru_lambda[l, dr], rev=bool(dr), **kw)
        y1 = lru(1, hf=lru(0))
        gla = lambda dr, **kw: _gla(p, wg[l, dr], gla_gate_b[l, dr], rev=bool(dr), **kw)
        y2 = gla(1, of=gla(0), norm_g=gla_norm_g[l])
        qn, kn, vn = _qkv(p, cos, sin, q_norm_g[l], k_norm_g[l])
        y3 = _attn(qn, kn, vn, ctx_len)
        x1 = _merge(xs, y1, y2, y3, p, mods[l], wb[l], wo[l], ctx_len)
        xs = _mlp(x1, mods[l], norm2_g[l], w1[l], w2[l], ctx_len)
    return xs[:, ctx_len:]
```

```python
import functools

import jax
import jax.numpy as jnp
from jax import lax
from jax.experimental import pallas as pl
from jax.experimental.pallas import tpu as pltpu

F32 = jnp.float32
BF16 = jnp.bfloat16

NORM_EPS = 1e-6
GRID_W = 64
ROPE_THETA = 10000.0
RGLRU_C = 8.0
LRU_BLOCK_W = 64
CONV_WIDTH = 4
GLA_HEADS = 4
GLA_DK = 128
GLA_DV = 256
GLA_GATE_RANK = 16
GLA_TAU = 16.0
GLA_CHUNK = 64
ATT_HEADS = 8
ATT_KV_HEADS = 2
ATT_GROUP = ATT_HEADS // ATT_KV_HEADS
HEAD_DIM = 128
N_BRANCHES = 3
Q_SCALE_LOG2 = HEAD_DIM ** -0.5 * 1.4426950408889634

LANES = 128
SUBLANES = 8
MXU_DIM = 256
VMEM_LIMIT_BYTES = 56 * 1024 * 1024

SEQ_BLOCK = 256
HALO = 2 * SUBLANES

D = 1024
P_LX, P_LY, P_GV, P_GR, P_AQ = 0, 1024, 2048, 3072, 4096
P_GQ, P_GK, P_BG = 5120, 5632, 6144
P_AK, P_AV, P_GZ = 9216, 9472, 9728
P_GZ_PAD = 256
P_WIDTH = P_GZ + P_GZ_PAD
PROJ_TN = 768


def _first_divisor(n, candidates):
    for c in candidates:
        if n % c == 0:
            return c
    raise ValueError(f"no tile in {candidates} divides {n}")


def _params(*sem):
    return pltpu.CompilerParams(dimension_semantics=sem, vmem_limit_bytes=VMEM_LIMIT_BYTES)


def _rms(x):
    return x * lax.rsqrt(jnp.mean(x * x, axis=-1, keepdims=True) + NORM_EPS)


def _dot(a, b):
    return jnp.dot(a, b, preferred_element_type=F32)


def _dot_nt(a, b):
    return lax.dot_general(a, b, (((1,), (1,)), ((), ())), preferred_element_type=F32)


def _dot_tn(a, b):
    return lax.dot_general(a, b, (((0,), (0,)), ((), ())), preferred_element_type=F32)


def _ctx_rows(tile_idx, tm, ctx_len):
    rows = tile_idx * tm + lax.broadcasted_iota(jnp.int32, (tm, 1), 0)
    return rows < ctx_len


def _mod_kernel(c_ref, w_ref, b_ref, o_ref):
    c = c_ref[...]
    cs = (c * jax.nn.sigmoid(c)).astype(BF16)
    o_ref[...] = _dot(cs, w_ref[...].astype(BF16)) + b_ref[...]


def _modulation(c, c_ctx, mod_w, mod_b):
    n_layers, d, n6 = mod_w.shape
    b = c.shape[0]
    rows = -(-(b + 1) // SUBLANES) * SUBLANES
    cs = jnp.zeros((rows, d), F32).at[:b].set(c).at[b].set(c_ctx)
    tn = _first_divisor(n6, (1536, 1024, 512))
    out = pl.pallas_call(
        _mod_kernel,
        grid=(n_layers, n6 // tn),
        in_specs=[
            pl.BlockSpec((rows, d), lambda l, j: (0, 0)),
            pl.BlockSpec((None, d, tn), lambda l, j: (l, 0, j)),
            pl.BlockSpec((None, 1, tn), lambda l, j: (l, 0, j)),
        ],
        out_specs=pl.BlockSpec((None, rows, tn), lambda l, j: (l, 0, j)),
        out_shape=jax.ShapeDtypeStruct((n_layers, rows, n6), F32),
        compiler_params=_params("parallel", "parallel"),
        name="modulation",
    )(cs, mod_w, mod_b.reshape(n_layers, 1, n6))
    return out.reshape(n_layers, rows, 6, d)[:, :b + 1]


def _proj_kernel(x_ref, modb_ref, modc_ref, g_ref, w_ref, o_ref, h_ref, *, tm, ctx_len):
    i = pl.program_id(1)

    @pl.when(pl.program_id(2) == 0)
    def _():
        y = _rms(x_ref[...]) * g_ref[...]
        is_ctx = _ctx_rows(i, tm, ctx_len)
        shift = jnp.where(is_ctx, modc_ref[0:1, :], modb_ref[0:1, :])
        scale = jnp.where(is_ctx, modc_ref[1:2, :], modb_ref[1:2, :])
        h_ref[...] = (y * (1.0 + scale) + shift).astype(BF16)

    o_ref[...] = _dot(h_ref[...], w_ref[...]).astype(o_ref.dtype)


def _proj(xs, mods, norm_g, wp, ctx_len):
    b, s, d = xs.shape
    tm = _first_divisor(s, (1408, 1280, 640, 256))
    nb = mods.shape[0] - 1
    return pl.pallas_call(
        functools.partial(_proj_kernel, tm=tm, ctx_len=ctx_len),
        grid=(b, s // tm, P_WIDTH // PROJ_TN),
        in_specs=[
            pl.BlockSpec((None, tm, d), lambda bi, i, j: (bi, i, 0)),
            pl.BlockSpec((None, 6, d), lambda bi, i, j: (bi, 0, 0)),
            pl.BlockSpec((None, 6, d), lambda bi, i, j: (nb, 0, 0)),
            pl.BlockSpec((1, d), lambda bi, i, j: (0, 0)),
            pl.BlockSpec((d, PROJ_TN), lambda bi, i, j: (0, j)),
        ],
        out_specs=pl.BlockSpec((None, tm, PROJ_TN), lambda bi, i, j: (bi, i, j)),
        out_shape=jax.ShapeDtypeStruct((b, s, P_WIDTH), BF16),
        scratch_shapes=[pltpu.VMEM((tm, d), BF16)],
        compiler_params=_params("parallel", "parallel", "arbitrary"),
        name="proj",
    )(xs, mods, mods, norm_g.reshape(1, d), wp)


def _seq_block(tb, nblk, rev):
    if not rev:
        return tb
    return jnp.where(tb == 0, 0, nblk - tb)


def _gelu_tanh(x):
    return 0.5 * x * (1.0 + jnp.tanh(0.7978845608028654 * (x + 0.044715 * (x * x * x))))


def _lru_kernel(*refs, rev, nblk):
    if rev:
        (lx_ref, prev_ref, next_ref, cw_ref, cb_ref, wa_ref, ab_ref, wx_ref, xb_ref, lam_ref,
         hf_ref, ly_ref, o_ref, carry_ref, xpad_ref) = refs
    else:
        (lx_ref, prev_ref, next_ref, cw_ref, cb_ref, wa_ref, ab_ref, wx_ref, xb_ref, lam_ref,
         o_ref, carry_ref, xpad_ref) = refs
    tb = pl.program_id(1)
    te = _seq_block(tb, nblk, rev)
    r, w = lx_ref.shape

    @pl.when(tb == 0)
    def _():
        carry_ref[...] = jnp.zeros_like(carry_ref)

    x = lx_ref[...].astype(F32)
    has_prev = te >= 2
    has_next = jnp.logical_and(te >= 1, te <= nblk - 2)
    xpad_ref[0:HALO, :] = jnp.where(has_prev, prev_ref[...].astype(F32), 0.0)
    xpad_ref[HALO:HALO + r, :] = x
    xpad_ref[HALO + r:2 * HALO + r, :] = jnp.where(has_next, next_ref[...].astype(F32), 0.0)
    xc = (cw_ref[0:1, :] * xpad_ref[HALO - 1:HALO - 1 + r, :]
          + cw_ref[1:2, :] * x
          + cw_ref[2:3, :] * xpad_ref[HALO + 1:HALO + 1 + r, :]
          + cw_ref[3:4, :] * xpad_ref[HALO + 2:HALO + 2 + r, :]
          + cb_ref[...])

    xcb = xc.astype(BF16)
    lam = lam_ref[...]
    neg_sp = -(jnp.maximum(-lam, 0.0) + jnp.log1p(jnp.exp(-jnp.abs(lam))))
    a_parts, u_parts = [], []
    for gi in range(w // MXU_DIM):
        sl = slice(gi * MXU_DIM, (gi + 1) * MXU_DIM)
        rg = jax.nn.sigmoid(_dot(xcb[:, sl], wa_ref[gi]) + ab_ref[:, sl])
        ig = jax.nn.sigmoid(_dot(xcb[:, sl], wx_ref[gi]) + xb_ref[:, sl])
        log_a = RGLRU_C * rg * neg_sp[:, sl]
        a_g = jnp.exp(log_a)
        a_parts.append(a_g)
        one_minus_a2 = -jnp.tanh(log_a) * (1.0 + a_g * a_g)
        u_parts.append(jnp.sqrt(one_minus_a2) * (ig * xc[:, sl]))
    a = jnp.concatenate(a_parts, axis=1)
    u = jnp.concatenate(u_parts, axis=1)

    row = lax.broadcasted_iota(jnp.int32, (SUBLANES, w), 0)
    carry = carry_ref[...]
    groups = range(r // SUBLANES)
    h_parts = [None] * len(groups)
    for g in (reversed(groups) if rev else groups):
        rows = slice(g * SUBLANES, (g + 1) * SUBLANES)
        ag, ug = a[rows], u[rows]
        for k in (1, 2, 4):
            if rev:
                shift, valid = SUBLANES - k, row < SUBLANES - k
            else:
                shift, valid = k, row >= k
            a_sh = pltpu.roll(ag, shift, 0)
            u_sh = pltpu.roll(ug, shift, 0)
            ug = ug + ag * jnp.where(valid, u_sh, 0.0)
            ag = ag * jnp.where(valid, a_sh, 1.0)
        h = ag * carry + ug
        h_parts[g] = h
        carry = h[0:1, :] if rev else h[SUBLANES - 1:SUBLANES, :]
    carry_ref[...] = carry
    h_all = jnp.concatenate(h_parts, axis=0)
    if rev:
        h_all = (hf_ref[...] + h_all) * _gelu_tanh(ly_ref[...].astype(F32))
    o_ref[...] = h_all.astype(o_ref.dtype)


def _lru(p, conv_w, conv_b, wa, ab, wx, xb, lam, rev, hf=None):
    b, s, _ = p.shape
    w = conv_w.shape[1]
    r = SEQ_BLOCK
    nblk = s // r
    per_halo = r // HALO
    last_halo = s // HALO - 1
    blk = lambda tb: _seq_block(tb, nblk, rev)
    ng = w // MXU_DIM
    in_specs = [
        pl.BlockSpec((None, r, w), lambda bi, tb: (bi, blk(tb), P_LX // w)),
        pl.BlockSpec((None, HALO, w), lambda bi, tb: (bi, jnp.maximum(blk(tb) * per_halo - 1, 0), P_LX // w)),
        pl.BlockSpec((None, HALO, w),
                     lambda bi, tb: (bi, jnp.minimum((blk(tb) + 1) * per_halo, last_halo), P_LX // w)),
        pl.BlockSpec((CONV_WIDTH, w), lambda bi, tb: (0, 0)),
        pl.BlockSpec((1, w), lambda bi, tb: (0, 0)),
        pl.BlockSpec((ng, MXU_DIM, MXU_DIM), lambda bi, tb: (0, 0, 0)),
        pl.BlockSpec((1, w), lambda bi, tb: (0, 0)),
        pl.BlockSpec((ng, MXU_DIM, MXU_DIM), lambda bi, tb: (0, 0, 0)),
        pl.BlockSpec((1, w), lambda bi, tb: (0, 0)),
        pl.BlockSpec((1, w), lambda bi, tb: (0, 0)),
    ]
    args = [p, p, p, conv_w, conv_b.reshape(1, w), wa, ab.reshape(1, w), wx, xb.reshape(1, w),
            lam.reshape(1, w)]
    if rev:
        in_specs += [
            pl.BlockSpec((None, r, w), lambda bi, tb: (bi, blk(tb), 0)),
            pl.BlockSpec((None, r, w), lambda bi, tb: (bi, blk(tb), P_LY // w)),
        ]
        args += [hf, p]
    return pl.pallas_call(
        functools.partial(_lru_kernel, rev=rev, nblk=nblk),
        grid=(b, nblk),
        in_specs=in_specs,
        out_specs=pl.BlockSpec((None, r, w), lambda bi, tb: (bi, blk(tb), 0)),
        out_shape=jax.ShapeDtypeStruct((b, s, w), BF16 if rev else F32),
        scratch_shapes=[pltpu.VMEM((1, w), F32), pltpu.VMEM((r + 2 * HALO, w), F32)],
        compiler_params=_params("arbitrary", "arbitrary"),
        name="lru_rev" if rev else "lru_fwd",
    )(*args)


def _split3(x):
    hi = x.astype(BF16)
    r1 = x - hi.astype(F32)
    mid = r1.astype(BF16)
    lo = (r1 - mid.astype(F32)).astype(BF16)
    return hi, mid, lo


def _gla_kernel(*refs, rev):
    if rev:
        q_ref, k_ref, v_ref, z_ref, wg_ref, gb_ref, of_ref, r_ref, ng_ref, o_ref, st_ref = refs
    else:
        q_ref, k_ref, v_ref, z_ref, wg_ref, gb_ref, o_ref, st_ref = refs
    r = q_ref.shape[0]
    c = GLA_CHUNK
    nch = r // c

    @pl.when(pl.program_id(1) == 0)
    def _():
        st_ref[...] = jnp.zeros_like(st_ref)

    zz = _dot(z_ref[...], wg_ref[...]) + gb_ref[...]
    g = (jnp.minimum(zz, 0.0) - jnp.log1p(jnp.exp(-jnp.abs(zz)))) * (1.0 / GLA_TAU)

    ri = lax.broadcasted_iota(jnp.int32, (r, r), 0)
    ci = lax.broadcasted_iota(jnp.int32, (r, r), 1)
    same = lax.shift_right_logical(ri, 6) == lax.shift_right_logical(ci, 6)
    tri = jnp.logical_and(same, (ci >= ri) if rev else (ci <= ri))
    tri_b = jnp.where(tri, 1.0, 0.0).astype(BF16)
    bcum = sum(_dot(tri_b, part) for part in _split3(g))

    q = q_ref[...].astype(F32) * (GLA_DK ** -0.5)
    k = k_ref[...].astype(F32)
    qd, kd, kl, qe, dec = [], [], [], [], []
    for ch in range(nch):
        rows = slice(ch * c, (ch + 1) * c)
        b_c = bcum[rows]
        if rev:
            last, mid = b_c[0:1], b_c[c // 2:c // 2 + 1]
        else:
            last, mid = b_c[c - 1:c], b_c[c // 2 - 1:c // 2]
        qd.append(q[rows] * jnp.exp(b_c - mid))
        kd.append(k[rows] * jnp.exp(mid - b_c))
        kl.append(k[rows] * jnp.exp(last - b_c))
        qe.append(q[rows] * jnp.exp(b_c))
        dec.append(jnp.exp(last))
    qd = jnp.concatenate(qd, axis=0).astype(BF16)
    kd = jnp.concatenate(kd, axis=0).astype(BF16)
    kl = jnp.concatenate(kl, axis=0).astype(BF16)
    qe = jnp.concatenate(qe, axis=0).astype(BF16)

    chunks = range(nch)
    for h in range(GLA_HEADS):
        hl = slice(h * GLA_DK, (h + 1) * GLA_DK)
        vl = slice(h * GLA_DV, (h + 1) * GLA_DV)
        vb = v_ref[:, vl]
        sc = jnp.where(tri, _dot_nt(qd[:, hl], kd[:, hl]), 0.0)
        o_h = _dot(sc.astype(BF16), vb)
        st = st_ref[h]
        o_parts = [None] * nch
        for ch in (reversed(chunks) if rev else chunks):
            rows = slice(ch * c, (ch + 1) * c)
            o_parts[ch] = o_h[rows] + _dot_nt(qe[rows, hl], st.astype(BF16))
            st = st * dec[ch][:, hl] + _dot_tn(vb[rows], kl[rows, hl])
        st_ref[h] = st
        o = jnp.concatenate(o_parts, axis=0)
        if rev:
            o = _rms(of_ref[:, vl] + o) * ng_ref[...]
            rg = r_ref[:, vl].astype(F32)
            o = o * (rg * jax.nn.sigmoid(rg))
        o_ref[:, vl] = o.astype(o_ref.dtype)


def _gla(p, wg, gb, rev, of=None, norm_g=None):
    b, s, _ = p.shape
    r = SEQ_BLOCK
    nblk = s // r
    dk = GLA_HEADS * GLA_DK
    dv = GLA_HEADS * GLA_DV
    blk = lambda tb: _seq_block(tb, nblk, rev)
    in_specs = [
        pl.BlockSpec((None, r, dk), lambda bi, tb: (bi, blk(tb), P_GQ // dk)),
        pl.BlockSpec((None, r, dk), lambda bi, tb: (bi, blk(tb), P_GK // dk)),
        pl.BlockSpec((None, r, dv), lambda bi, tb: (bi, blk(tb), P_GV // dv)),
        pl.BlockSpec((None, r, LANES), lambda bi, tb: (bi, blk(tb), P_GZ // LANES)),
        pl.BlockSpec((LANES, dk), lambda bi, tb: (0, 0)),
        pl.BlockSpec((1, dk), lambda bi, tb: (0, 0)),
    ]
    args = [p, p, p, p, wg, gb.reshape(1, dk)]
    if rev:
        in_specs += [
            pl.BlockSpec((None, r, dv), lambda bi, tb: (bi, blk(tb), 0)),
            pl.BlockSpec((None, r, dv), lambda bi, tb: (bi, blk(tb), P_GR // dv)),
            pl.BlockSpec((1, GLA_DV), lambda bi, tb: (0, 0)),
        ]
        args += [of, p, norm_g.reshape(1, GLA_DV)]
    return pl.pallas_call(
        functools.partial(_gla_kernel, rev=rev),
        grid=(b, nblk),
        in_specs=in_specs,
        out_specs=pl.BlockSpec((None, r, dv), lambda bi, tb: (bi, blk(tb), 0)),
        out_shape=jax.ShapeDtypeStruct((b, s, dv), BF16 if rev else F32),
        scratch_shapes=[pltpu.VMEM((GLA_HEADS, GLA_DV, GLA_DK), F32)],
        compiler_params=_params("arbitrary", "arbitrary"),
        name="gla_rev" if rev else "gla_fwd",
    )(*args)


def _head_perm():
    quarter = HEAD_DIM // 4
    idx = jnp.arange(HEAD_DIM).reshape(4, quarter)
    return jnp.concatenate([idx[0], idx[2], idx[1], idx[3]])


def _rope_tables(n_lat, ctx_len):
    half = HEAD_DIM // 4
    t = jnp.arange(n_lat)
    inv_freq = ROPE_THETA ** (-jnp.arange(0, 2 * half, 2, dtype=F32) / (2 * half))
    ang_r = (t // GRID_W).astype(F32)[:, None] * inv_freq[None, :]
    ang_c = (t % GRID_W).astype(F32)[:, None] * inv_freq[None, :]
    cos = jnp.concatenate([jnp.cos(ang_r), jnp.cos(ang_c)] * 2, axis=-1)
    sin = jnp.concatenate([-jnp.sin(ang_r), -jnp.sin(ang_c), jnp.sin(ang_r), jnp.sin(ang_c)], axis=-1)
    cos = jnp.concatenate([jnp.ones((ctx_len, HEAD_DIM), F32), cos], axis=0)
    sin = jnp.concatenate([jnp.zeros((ctx_len, HEAD_DIM), F32), sin], axis=0)
    return cos, sin


def _qkv_kernel(aq_ref, ak_ref, av_ref, cos_ref, sin_ref, qg_ref, kg_ref, q_ref, k_ref, v_ref):
    tm = aq_ref.shape[0]
    cos, sin = cos_ref[...], sin_ref[...]

    def norm_rope(x, g):
        y = _rms(x.astype(F32)) * g
        return y * cos + pltpu.roll(y, HEAD_DIM // 2, 1) * sin

    for h in range(ATT_HEADS):
        sl = slice(h * HEAD_DIM, (h + 1) * HEAD_DIM)
        q_ref[:, sl] = (norm_rope(aq_ref[:, sl], qg_ref[...]) * Q_SCALE_LOG2).astype(BF16)
    for h in range(ATT_KV_HEADS):
        sl = slice(h * HEAD_DIM, (h + 1) * HEAD_DIM)
        k_ref[:, sl] = norm_rope(ak_ref[:, sl], kg_ref[...]).astype(BF16)
        v_ref[:, 2 * h * HEAD_DIM:(2 * h + 1) * HEAD_DIM] = av_ref[:, sl]
        v_ref[:, (2 * h + 1) * HEAD_DIM:(2 * h + 2) * HEAD_DIM] = jnp.ones((tm, HEAD_DIM), BF16)


def _qkv(p, cos, sin, q_norm_g, k_norm_g):
    b, s, _ = p.shape
    tm = _first_divisor(s, (768, 256))
    dq = ATT_HEADS * HEAD_DIM
    dkv = ATT_KV_HEADS * HEAD_DIM
    return pl.pallas_call(
        _qkv_kernel,
        grid=(b, s // tm),
        in_specs=[
            pl.BlockSpec((None, tm, dq), lambda bi, i: (bi, i, P_AQ // dq)),
            pl.BlockSpec((None, tm, dkv), lambda bi, i: (bi, i, P_AK // dkv)),
            pl.BlockSpec((None, tm, dkv), lambda bi, i: (bi, i, P_AV // dkv)),
            pl.BlockSpec((tm, HEAD_DIM), lambda bi, i: (i, 0)),
            pl.BlockSpec((tm, HEAD_DIM), lambda bi, i: (i, 0)),
            pl.BlockSpec((1, HEAD_DIM), lambda bi, i: (0, 0)),
            pl.BlockSpec((1, HEAD_DIM), lambda bi, i: (0, 0)),
        ],
        out_specs=[
            pl.BlockSpec((None, tm, dq), lambda bi, i: (bi, i, 0)),
            pl.BlockSpec((None, tm, dkv), lambda bi, i: (bi, i, 0)),
            pl.BlockSpec((None, tm, 2 * dkv), lambda bi, i: (bi, i, 0)),
        ],
        out_shape=[
            jax.ShapeDtypeStruct((b, s, dq), BF16),
            jax.ShapeDtypeStruct((b, s, dkv), BF16),
            jax.ShapeDtypeStruct((b, s, 2 * dkv), BF16),
        ],
        compiler_params=_params("parallel", "parallel"),
        name="qkv",
    )(p, p, p, cos, sin, q_norm_g.reshape(1, HEAD_DIM), k_norm_g.reshape(1, HEAD_DIM))


def _attn_scores(qs_ref, k_ref, start, size):
    return _dot_nt(qs_ref[...], k_ref[pl.ds(start, size), :])


def _attn_consume(s, v, m_ref, acc_ref):
    reps = s.shape[1] // LANES
    m_prev = m_ref[...]
    m_new = jnp.maximum(m_prev, jnp.max(s, axis=-1, keepdims=True))
    p = jnp.exp2(s - jnp.concatenate([m_new] * reps, axis=1))
    alpha = jnp.exp2(m_prev - m_new)
    acc_ref[...] = jnp.concatenate([alpha] * 2, axis=1) * acc_ref[...] + _dot(p.astype(BF16), v)
    m_ref[...] = m_new


def _attn_kernel(q_ref, k_ref, v_ref, o_ref, qs_ref, m_ref, acc_ref, sa_ref, sb_ref, *, ctx_len, ck):
    tq = q_ref.shape[0]
    nck = k_ref.shape[0] // ck
    is_ctx_tile = pl.program_id(2) == 0
    for h in range(ATT_GROUP):
        qs_ref[h * tq:(h + 1) * tq, :] = q_ref[:, h * HEAD_DIM:(h + 1) * HEAD_DIM]
    m_ref[...] = jnp.full_like(m_ref, -jnp.inf)
    acc_ref[...] = jnp.zeros_like(acc_ref)

    @pl.when(is_ctx_tile)
    def _():
        _attn_consume(_attn_scores(qs_ref, k_ref, 0, ctx_len), v_ref[0:ctx_len, :], m_ref, acc_ref)

    @pl.when(jnp.logical_not(is_ctx_tile))
    def _():
        def chunk(c):
            return pl.multiple_of(c * ck, LANES)

        sa_ref[...] = _attn_scores(qs_ref, k_ref, 0, ck)

        def body(i, carry):
            sb_ref[...] = _attn_scores(qs_ref, k_ref, chunk(2 * i + 1), ck)
            _attn_consume(sa_ref[...], v_ref[pl.ds(chunk(2 * i), ck), :], m_ref, acc_ref)
            sa_ref[...] = _attn_scores(qs_ref, k_ref, chunk(2 * i + 2), ck)
            _attn_consume(sb_ref[...], v_ref[pl.ds(chunk(2 * i + 1), ck), :], m_ref, acc_ref)
            return carry

        pairs = (nck - 1) // 2
        lax.fori_loop(0, pairs, body, 0)
        if nck - 1 - 2 * pairs:
            sb_ref[...] = _attn_scores(qs_ref, k_ref, (nck - 1) * ck, ck)
            _attn_consume(sa_ref[...], v_ref[(nck - 2) * ck:(nck - 1) * ck, :], m_ref, acc_ref)
            _attn_consume(sb_ref[...], v_ref[(nck - 1) * ck:nck * ck, :], m_ref, acc_ref)
        else:
            _attn_consume(sa_ref[...], v_ref[(nck - 1) * ck:nck * ck, :], m_ref, acc_ref)

    acc = acc_ref[...]
    out = acc[:, :HEAD_DIM] / acc[:, HEAD_DIM:]
    for h in range(ATT_GROUP):
        o_ref[:, h * HEAD_DIM:(h + 1) * HEAD_DIM] = out[h * tq:(h + 1) * tq].astype(o_ref.dtype)


def _attn(qn, kn, vn, ctx_len):
    b, s, dq = qn.shape
    tq = SEQ_BLOCK
    ck = _first_divisor(s, (768, 640, 256))
    gw = ATT_GROUP * HEAD_DIM
    rows = ATT_GROUP * tq
    return pl.pallas_call(
        functools.partial(_attn_kernel, ctx_len=ctx_len, ck=ck),
        grid=(b, ATT_KV_HEADS, s // tq),
        in_specs=[
            pl.BlockSpec((None, tq, gw), lambda bi, h, i: (bi, i, h)),
            pl.BlockSpec((None, s, HEAD_DIM), lambda bi, h, i: (bi, 0, h)),
            pl.BlockSpec((None, s, 2 * HEAD_DIM), lambda bi, h, i: (bi, 0, h)),
        ],
        out_specs=pl.BlockSpec((None, tq, gw), lambda bi, h, i: (bi, i, h)),
        out_shape=jax.ShapeDtypeStruct((b, s, dq), BF16),
        scratch_shapes=[
            pltpu.VMEM((rows, HEAD_DIM), BF16),
            pltpu.VMEM((rows, LANES), F32),
            pltpu.VMEM((rows, 2 * HEAD_DIM), F32),
            pltpu.VMEM((rows, ck), F32),
            pltpu.VMEM((rows, ck), F32),
        ],
        compiler_params=_params("parallel", "parallel", "arbitrary"),
        name="attn",
    )(qn, kn, vn)


def _merge_kernel(x_ref, y1_ref, y2_ref, y3_ref, bg_ref, modb_ref, modc_ref, wb_ref, wo_ref, o_ref,
                  *, tm, ctx_len):
    d = x_ref.shape[1]
    gate = jax.nn.sigmoid(bg_ref[...].astype(F32))
    m = (gate[:, 0:d] * _dot(y1_ref[...], wb_ref[0])
         + gate[:, d:2 * d] * _dot(y2_ref[...], wb_ref[1])
         + gate[:, 2 * d:3 * d] * _dot(y3_ref[...], wb_ref[2]))
    upd = _dot(m.astype(BF16), wo_ref[...])
    is_ctx = _ctx_rows(pl.program_id(1), tm, ctx_len)
    ga = jnp.where(is_ctx, modc_ref[2:3, :], modb_ref[2:3, :])
    o_ref[...] = x_ref[...] + ga * upd


def _merge(xs, y1, y2, y3, p, mods, wb, wo, ctx_len):
    b, s, d = xs.shape
    tm = _first_divisor(s, (768, 640, 256))
    nb = mods.shape[0] - 1
    row = lambda bi, i: (bi, i, 0)
    return pl.pallas_call(
        functools.partial(_merge_kernel, tm=tm, ctx_len=ctx_len),
        grid=(b, s // tm),
        in_specs=[
            pl.BlockSpec((None, tm, d), row),
            pl.BlockSpec((None, tm, d), row),
            pl.BlockSpec((None, tm, d), row),
            pl.BlockSpec((None, tm, d), row),
            pl.BlockSpec((None, tm, N_BRANCHES * d), lambda bi, i: (bi, i, P_BG // (N_BRANCHES * d))),
            pl.BlockSpec((None, 6, d), lambda bi, i: (bi, 0, 0)),
            pl.BlockSpec((None, 6, d), lambda bi, i: (nb, 0, 0)),
            pl.BlockSpec((N_BRANCHES, d, d), lambda bi, i: (0, 0, 0)),
            pl.BlockSpec((d, d), lambda bi, i: (0, 0)),
        ],
        out_specs=pl.BlockSpec((None, tm, d), row),
        out_shape=jax.ShapeDtypeStruct((b, s, d), F32),
        compiler_params=_params("parallel", "parallel"),
        name="merge",
    )(xs, y1, y2, y3, p, mods, mods, wb, wo)


def _mlp_kernel(x_ref, modb_ref, modc_ref, g_ref, w1_ref, w2_ref, o_ref, h_ref, acc_ref, *, tm, ctx_len):
    i = pl.program_id(1)
    j = pl.program_id(2)
    is_ctx = _ctx_rows(i, tm, ctx_len)

    @pl.when(j == 0)
    def _():
        y = _rms(x_ref[...]) * g_ref[...]
        shift = jnp.where(is_ctx, modc_ref[3:4, :], modb_ref[3:4, :])
        scale = jnp.where(is_ctx, modc_ref[4:5, :], modb_ref[4:5, :])
        h_ref[...] = (y * (1.0 + scale) + shift).astype(BF16)
        acc_ref[...] = jnp.zeros_like(acc_ref)

    t = jnp.maximum(_dot(h_ref[...], w1_ref[...]), 0.0)
    acc_ref[...] += _dot((t * t).astype(BF16), w2_ref[...])

    @pl.when(j == pl.num_programs(2) - 1)
    def _():
        ga = jnp.where(is_ctx, modc_ref[5:6, :], modb_ref[5:6, :])
        o_ref[...] = x_ref[...] + ga * acc_ref[...]


def _mlp(xs, mods, norm_g, w1, w2, ctx_len):
    b, s, d = xs.shape
    dff = w1.shape[1]
    tm = _first_divisor(s, (1408, 1280, 640, 256))
    tf = _first_divisor(dff, (1024, 512))
    nb = mods.shape[0] - 1
    return pl.pallas_call(
        functools.partial(_mlp_kernel, tm=tm, ctx_len=ctx_len),
        grid=(b, s // tm, dff // tf),
        in_specs=[
            pl.BlockSpec((None, tm, d), lambda bi, i, j: (bi, i, 0)),
            pl.BlockSpec((None, 6, d), lambda bi, i, j: (bi, 0, 0)),
            pl.BlockSpec((None, 6, d), lambda bi, i, j: (nb, 0, 0)),
            pl.BlockSpec((1, d), lambda bi, i, j: (0, 0)),
            pl.BlockSpec((d, tf), lambda bi, i, j: (0, j)),
            pl.BlockSpec((tf, d), lambda bi, i, j: (j, 0)),
        ],
        out_specs=pl.BlockSpec((None, tm, d), lambda bi, i, j: (bi, i, 0)),
        out_shape=jax.ShapeDtypeStruct((b, s, d), F32),
        scratch_shapes=[pltpu.VMEM((tm, d), BF16), pltpu.VMEM((tm, d), F32)],
        compiler_params=_params("parallel", "parallel", "arbitrary"),
        name="mlp",
    )(xs, mods, mods, norm_g.reshape(1, d), w1, w2)


def _pack_w_in(w_in):
    lx, ly, gq, gk, gv, gr, gz, aq, ak, av, bg = jnp.split(
        w_in, (1024, 2048, 2560, 3072, 4096, 5120, 5152, 6176, 6432, 6688), axis=-1)
    pad = jnp.zeros(w_in.shape[:-1] + (P_GZ_PAD - gz.shape[-1],), w_in.dtype)
    perm = _head_perm()

    def permute_heads(w):
        heads = w.reshape(w.shape[:-1] + (w.shape[-1] // HEAD_DIM, HEAD_DIM))
        return heads[..., perm].reshape(w.shape)

    return jnp.concatenate([lx, ly, gv, gr, permute_heads(aq), gq, gk, bg, permute_heads(ak), av, gz, pad],
                           axis=-1).astype(BF16)


def _block_diag(w):
    per = MXU_DIM // LRU_BLOCK_W
    lead = w.shape[:-3]
    wg = w.reshape(lead + (w.shape[-3] // per, per, LRU_BLOCK_W, LRU_BLOCK_W))
    eye = jnp.eye(per, dtype=w.dtype)
    bd = jnp.einsum("...gaij,ac->...gaicj", wg, eye)
    return bd.reshape(lead + (w.shape[-3] // per, MXU_DIM, MXU_DIM)).astype(BF16)


def _pad_gate_w(gate_w):
    n_layers, n_dir, rank, dk = gate_w.shape
    out = jnp.zeros((n_layers, n_dir, LANES, dk), gate_w.dtype)
    for d in range(n_dir):
        out = out.at[:, d, d * rank:(d + 1) * rank].set(gate_w[:, d])
    return out.astype(BF16)


def kernel(x, c, ctx, c_ctx, mod_w, mod_b, norm1_g, norm2_g, w_in, conv_w, conv_b, lru_a_w, lru_a_b,
           lru_x_w, lru_x_b, lru_lambda, gla_gate_w, gla_gate_b, gla_norm_g, q_norm_g, k_norm_g,
           w_branch, w_out, mlp_w1, mlp_w2):
    n_lat = x.shape[1]
    ctx_len = ctx.shape[1]
    assert x.shape[2] == D and ctx_len == SEQ_BLOCK and n_lat % SEQ_BLOCK == 0
    n_layers = mod_w.shape[0]

    xs = jnp.concatenate([ctx, x], axis=1)
    mods = _modulation(c, c_ctx, mod_w, mod_b)
    cos, sin = _rope_tables(n_lat, ctx_len)
    perm = _head_perm()
    wp = _pack_w_in(w_in)
    wa = _block_diag(lru_a_w)
    wx = _block_diag(lru_x_w)
    wg = _pad_gate_w(gla_gate_w)
    wb = w_branch.astype(BF16)
    wo = w_out.astype(BF16)
    w1 = mlp_w1.astype(BF16)
    w2 = mlp_w2.astype(BF16)

    for l in range(n_layers):
        p = _proj(xs, mods[l], norm1_g[l], wp[l], ctx_len)
        lru = lambda dr, **kw: _lru(p, conv_w[l], conv_b[l], wa[l, dr], lru_a_b[l, dr], wx[l, dr],
                                    lru_x_b[l, dr], lru_lambda[l, dr], rev=bool(dr), **kw)
        y1 = lru(1, hf=lru(0))
        gla = lambda dr, **kw: _gla(p, wg[l, dr], gla_gate_b[l, dr], rev=bool(dr), **kw)
        y2 = gla(1, of=gla(0), norm_g=gla_norm_g[l])
        qn, kn, vn = _qkv(p, cos, sin, q_norm_g[l][perm], k_norm_g[l][perm])
        y3 = _attn(qn, kn, vn, ctx_len)
        x1 = _merge(xs, y1, y2, y3, p, mods[l], wb[l], wo[l], ctx_len)
        xs = _mlp(x1, mods[l], norm2_g[l], w1[l], w2[l], ctx_len)
    return xs[:, ctx_len:]
```

```python
import functools

import jax
import jax.numpy as jnp
from jax import lax
from jax.experimental import pallas as pl
from jax.experimental.pallas import tpu as pltpu

F32 = jnp.float32
BF16 = jnp.bfloat16

NORM_EPS = 1e-6
F32_TINY = 1e-37
GRID_W = 64
ROPE_THETA = 10000.0
RGLRU_C = 8.0
LRU_BLOCK_W = 64
CONV_WIDTH = 4
GLA_HEADS = 4
GLA_DK = 128
GLA_DV = 256
GLA_GATE_RANK = 16
GLA_TAU = 16.0
GLA_CHUNK = 64
ATT_HEADS = 8
ATT_KV_HEADS = 2
ATT_GROUP = ATT_HEADS // ATT_KV_HEADS
HEAD_DIM = 128
N_BRANCHES = 3
Q_SCALE_LOG2 = HEAD_DIM ** -0.5 * 1.4426950408889634

LANES = 128
SUBLANES = 8
MXU_DIM = 256
VMEM_LIMIT_BYTES = 56 * 1024 * 1024

SEQ_BLOCK = 256
HALO = 2 * SUBLANES

D = 1024
P_LX, P_LY, P_GV, P_GR, P_AQ = 0, 1024, 2048, 3072, 4096
P_GQ, P_GK, P_BG = 5120, 5632, 6144
P_AK, P_AV, P_GZ = 9216, 9472, 9728
P_GZ_PAD = 256
P_WIDTH = P_GZ + P_GZ_PAD
PROJ_TN = 768


def _first_divisor(n, candidates):
    for c in candidates:
        if n % c == 0:
            return c
    raise ValueError(f"no tile in {candidates} divides {n}")


def _params(*sem):
    return pltpu.CompilerParams(dimension_semantics=sem, vmem_limit_bytes=VMEM_LIMIT_BYTES)


def _rms(x):
    return x * lax.rsqrt(jnp.mean(x * x, axis=-1, keepdims=True) + NORM_EPS)


def _dot(a, b):
    return jnp.dot(a, b, preferred_element_type=F32)


def _dot_nt(a, b):
    return lax.dot_general(a, b, (((1,), (1,)), ((), ())), preferred_element_type=F32)


def _dot_tn(a, b):
    return lax.dot_general(a, b, (((0,), (0,)), ((), ())), preferred_element_type=F32)


def _ctx_rows(tile_idx, tm, ctx_len):
    rows = tile_idx * tm + lax.broadcasted_iota(jnp.int32, (tm, 1), 0)
    return rows < ctx_len


def _mod_kernel(c_ref, w_ref, b_ref, o_ref):
    c = c_ref[...]
    cs = (c * jax.nn.sigmoid(c)).astype(BF16)
    o_ref[...] = _dot(cs, w_ref[...].astype(BF16)) + b_ref[...]


def _modulation(c, c_ctx, mod_w, mod_b):
    n_layers, d, n6 = mod_w.shape
    b = c.shape[0]
    rows = -(-(b + 1) // SUBLANES) * SUBLANES
    cs = jnp.zeros((rows, d), F32).at[:b].set(c).at[b].set(c_ctx)
    tn = _first_divisor(n6, (1536, 1024, 512))
    out = pl.pallas_call(
        _mod_kernel,
        grid=(n_layers, n6 // tn),
        in_specs=[
            pl.BlockSpec((rows, d), lambda l, j: (0, 0)),
            pl.BlockSpec((None, d, tn), lambda l, j: (l, 0, j)),
            pl.BlockSpec((None, 1, tn), lambda l, j: (l, 0, j)),
        ],
        out_specs=pl.BlockSpec((None, rows, tn), lambda l, j: (l, 0, j)),
        out_shape=jax.ShapeDtypeStruct((n_layers, rows, n6), F32),
        compiler_params=_params("parallel", "parallel"),
        name="modulation",
    )(cs, mod_w, mod_b.reshape(n_layers, 1, n6))
    return out.reshape(n_layers, rows, 6, d)[:, :b + 1]


def _proj_kernel(x_ref, modb_ref, modc_ref, g_ref, w_ref, o_ref, h_ref, *, tm, ctx_len):
    i = pl.program_id(1)

    @pl.when(pl.program_id(2) == 0)
    def _():
        y = _rms(x_ref[...]) * g_ref[...]
        is_ctx = _ctx_rows(i, tm, ctx_len)
        shift = jnp.where(is_ctx, modc_ref[0:1, :], modb_ref[0:1, :])
        scale = jnp.where(is_ctx, modc_ref[1:2, :], modb_ref[1:2, :])
        h_ref[...] = (y * (1.0 + scale) + shift).astype(BF16)

    o_ref[...] = _dot(h_ref[...], w_ref[...]).astype(o_ref.dtype)


def _proj(xs, mods, norm_g, wp, ctx_len):
    b, s, d = xs.shape
    tm = _first_divisor(s, (1408, 1280, 640, 256))
    nb = mods.shape[0] - 1
    return pl.pallas_call(
        functools.partial(_proj_kernel, tm=tm, ctx_len=ctx_len),
        grid=(b, s // tm, P_WIDTH // PROJ_TN),
        in_specs=[
            pl.BlockSpec((None, tm, d), lambda bi, i, j: (bi, i, 0)),
            pl.BlockSpec((None, 6, d), lambda bi, i, j: (bi, 0, 0)),
            pl.BlockSpec((None, 6, d), lambda bi, i, j: (nb, 0, 0)),
            pl.BlockSpec((1, d), lambda bi, i, j: (0, 0)),
            pl.BlockSpec((d, PROJ_TN), lambda bi, i, j: (0, j)),
        ],
        out_specs=pl.BlockSpec((None, tm, PROJ_TN), lambda bi, i, j: (bi, i, j)),
        out_shape=jax.ShapeDtypeStruct((b, s, P_WIDTH), BF16),
        scratch_shapes=[pltpu.VMEM((tm, d), BF16)],
        compiler_params=_params("parallel", "parallel", "arbitrary"),
        name="proj",
    )(xs, mods, mods, norm_g.reshape(1, d), wp)


def _seq_block(tb, nblk, rev):
    if not rev:
        return tb
    return jnp.where(tb == 0, 0, nblk - tb)


def _per_sample(block_fn, shared):
    def body(*refs, **kw):
        for bb in range(refs[0].shape[0]):
            block_fn(*[r if i in shared else r.at[bb] for i, r in enumerate(refs)], **kw)
    return body


BATCH_TILE = (2, 1)


def _gelu_tanh(x):
    return 0.5 * x * (1.0 + jnp.tanh(0.7978845608028654 * (x + 0.044715 * (x * x * x))))


def _segment_perm(r, inverse):
    seg = r // SUBLANES
    ri = lax.broadcasted_iota(jnp.int32, (r, r), 0)
    ci = lax.broadcasted_iota(jnp.int32, (r, r), 1)
    if inverse:
        ri, ci = ci, ri
    src = (ri & (SUBLANES - 1)) * seg + lax.shift_right_logical(ri, 3)
    return jnp.where(ci == src, 1.0, 0.0).astype(BF16)


def _sublane_scan(a, u, carry, sub, rev):
    for k in (1, 2, 4):
        if rev:
            shift, valid = SUBLANES - k, sub < SUBLANES - k
        else:
            shift, valid = k, sub >= k
        a_sh = pltpu.roll(a, shift, 0)
        u_sh = pltpu.roll(u, shift, 0)
        u = u + a * jnp.where(valid, u_sh, 0.0)
        a = a * jnp.where(valid, a_sh, 1.0)
    return a * carry + u


def _lru_kernel(*refs, rev, nblk):
    if rev:
        (lx_ref, prev_ref, next_ref, cw_ref, cb_ref, wa_ref, ab_ref, wx_ref, xb_ref, lam_ref,
         hf_ref, ly_ref, o_ref, carry_ref) = refs
    else:
        (lx_ref, prev_ref, next_ref, cw_ref, cb_ref, wa_ref, ab_ref, wx_ref, xb_ref, lam_ref,
         o_ref, carry_ref) = refs
    tb = pl.program_id(1)
    te = _seq_block(tb, nblk, rev)
    r, w = lx_ref.shape
    seg = r // SUBLANES
    top = SUBLANES - 1

    @pl.when(tb == 0)
    def _():
        carry_ref[...] = jnp.zeros_like(carry_ref)

    perm = _segment_perm(r, inverse=False)
    x = _dot(perm, lx_ref[...])
    sub = lax.broadcasted_iota(jnp.int32, (SUBLANES, w), 0)

    has_prev = te >= 2
    has_next = jnp.logical_and(te >= 1, te <= nblk - 2)
    prev_rows = jnp.where(has_prev, prev_ref[...].astype(F32), 0.0)
    next_rows = jnp.where(has_next, next_ref[...].astype(F32), 0.0)
    edge_m1 = jnp.where(sub == 0, prev_rows[HALO - 1:HALO], pltpu.roll(x[r - SUBLANES:r], 1, 0))
    edge_p1 = jnp.where(sub == top, next_rows[0:1], pltpu.roll(x[0:SUBLANES], top, 0))
    edge_p2 = jnp.where(sub == top, next_rows[1:2], pltpu.roll(x[SUBLANES:2 * SUBLANES], top, 0))
    x_m1 = jnp.concatenate([edge_m1, x[:r - SUBLANES]], axis=0)
    x_p1 = jnp.concatenate([x[SUBLANES:], edge_p1], axis=0)
    x_p2 = jnp.concatenate([x[2 * SUBLANES:], edge_p1, edge_p2], axis=0)
    xc = (cw_ref[0:1, :] * x_m1 + cw_ref[1:2, :] * x + cw_ref[2:3, :] * x_p1 + cw_ref[3:4, :] * x_p2
          + cb_ref[...])

    xcb = xc.astype(BF16)
    lam = lam_ref[...]
    log_a_scale = -RGLRU_C * (jnp.maximum(-lam, 0.0) + jnp.log1p(jnp.exp(-jnp.abs(lam))))
    a_parts, u_parts = [], []
    for gi in range(w // MXU_DIM):
        sl = slice(gi * MXU_DIM, (gi + 1) * MXU_DIM)
        rg = jax.nn.sigmoid(_dot(xcb[:, sl], wa_ref[gi]) + ab_ref[:, sl])
        ig = jax.nn.sigmoid(_dot(xcb[:, sl], wx_ref[gi]) + xb_ref[:, sl])
        log_a = rg * log_a_scale[:, sl]
        a_g = jnp.exp(log_a)
        a_parts.append(a_g)
        one_minus_a2 = -jnp.tanh(log_a) * (1.0 + a_g * a_g)
        root = one_minus_a2 * lax.rsqrt(jnp.maximum(one_minus_a2, F32_TINY))
        u_parts.append(root * (ig * xc[:, sl]))
    a = jnp.concatenate(a_parts, axis=1)
    u = jnp.concatenate(u_parts, axis=1)

    steps = range(seg)
    h_loc = [None] * seg
    p_cum = [None] * seg
    hl = pc = None
    for i in (reversed(steps) if rev else steps):
        rows = slice(i * SUBLANES, (i + 1) * SUBLANES)
        hl = u[rows] if hl is None else a[rows] * hl + u[rows]
        pc = a[rows] if pc is None else a[rows] * pc
        h_loc[i], p_cum[i] = hl, pc
    carry = carry_ref[...]
    h_edge = _sublane_scan(pc, hl, carry, sub, rev)
    if rev:
        seg_in = jnp.where(sub == top, carry, pltpu.roll(h_edge, top, 0))
        carry_ref[...] = h_edge[0:1, :]
    else:
        seg_in = jnp.where(sub == 0, carry, pltpu.roll(h_edge, 1, 0))
        carry_ref[...] = h_edge[top:SUBLANES, :]
    h_all = jnp.concatenate([h_loc[i] + p_cum[i] * seg_in for i in steps], axis=0)
    if rev:
        gate = _gelu_tanh(_dot(perm, ly_ref[...]))
        y = ((hf_ref[...] + h_all) * gate).astype(BF16)
        o_ref[...] = _dot(_segment_perm(r, inverse=True), y).astype(o_ref.dtype)
    else:
        o_ref[...] = h_all


def _lru(p, conv_w, conv_b, wa, ab, wx, xb, lam, rev, hf=None):
    b, s, _ = p.shape
    w = conv_w.shape[1]
    r = SEQ_BLOCK
    nblk = s // r
    per_halo = r // HALO
    last_halo = s // HALO - 1
    blk = lambda tb: _seq_block(tb, nblk, rev)
    ng = w // MXU_DIM
    bt = _first_divisor(b, BATCH_TILE)
    in_specs = [
        pl.BlockSpec((bt, r, w), lambda bi, tb: (bi, blk(tb), P_LX // w)),
        pl.BlockSpec((bt, HALO, w), lambda bi, tb: (bi, jnp.maximum(blk(tb) * per_halo - 1, 0), P_LX // w)),
        pl.BlockSpec((bt, HALO, w),
                     lambda bi, tb: (bi, jnp.minimum((blk(tb) + 1) * per_halo, last_halo), P_LX // w)),
        pl.BlockSpec((CONV_WIDTH, w), lambda bi, tb: (0, 0)),
        pl.BlockSpec((1, w), lambda bi, tb: (0, 0)),
        pl.BlockSpec((ng, MXU_DIM, MXU_DIM), lambda bi, tb: (0, 0, 0)),
        pl.BlockSpec((1, w), lambda bi, tb: (0, 0)),
        pl.BlockSpec((ng, MXU_DIM, MXU_DIM), lambda bi, tb: (0, 0, 0)),
        pl.BlockSpec((1, w), lambda bi, tb: (0, 0)),
        pl.BlockSpec((1, w), lambda bi, tb: (0, 0)),
    ]
    args = [p, p, p, conv_w, conv_b.reshape(1, w), wa, ab.reshape(1, w), wx, xb.reshape(1, w),
            lam.reshape(1, w)]
    if rev:
        in_specs += [
            pl.BlockSpec((bt, r, w), lambda bi, tb: (bi, blk(tb), 0)),
            pl.BlockSpec((bt, r, w), lambda bi, tb: (bi, blk(tb), P_LY // w)),
        ]
        args += [hf, p]
    return pl.pallas_call(
        functools.partial(_per_sample(_lru_kernel, shared=range(3, 10)), rev=rev, nblk=nblk),
        grid=(b // bt, nblk),
        in_specs=in_specs,
        out_specs=pl.BlockSpec((bt, r, w), lambda bi, tb: (bi, blk(tb), 0)),
        out_shape=jax.ShapeDtypeStruct((b, s, w), BF16 if rev else F32),
        scratch_shapes=[pltpu.VMEM((bt, 1, w), F32)],
        compiler_params=_params("arbitrary", "arbitrary"),
        name="lru_rev" if rev else "lru_fwd",
    )(*args)


def _split3(x):
    hi = x.astype(BF16)
    r1 = x - hi.astype(F32)
    mid = r1.astype(BF16)
    lo = (r1 - mid.astype(F32)).astype(BF16)
    return hi, mid, lo


def _gla_kernel(*refs, rev):
    if rev:
        q_ref, k_ref, v_ref, z_ref, wg_ref, gb_ref, of_ref, r_ref, ng_ref, o_ref, st_ref = refs
    else:
        q_ref, k_ref, v_ref, z_ref, wg_ref, gb_ref, o_ref, st_ref = refs
    r = q_ref.shape[0]
    c = GLA_CHUNK
    nch = r // c

    @pl.when(pl.program_id(1) == 0)
    def _():
        st_ref[...] = jnp.zeros_like(st_ref)

    zz = _dot(z_ref[...], wg_ref[...]) + gb_ref[...]
    g = (jnp.minimum(zz, 0.0) - jnp.log1p(jnp.exp(-jnp.abs(zz)))) * (1.0 / GLA_TAU)

    ri = lax.broadcasted_iota(jnp.int32, (r, r), 0)
    ci = lax.broadcasted_iota(jnp.int32, (r, r), 1)
    same = lax.shift_right_logical(ri, 6) == lax.shift_right_logical(ci, 6)
    tri = jnp.logical_and(same, (ci >= ri) if rev else (ci <= ri))
    tri_b = jnp.where(tri, 1.0, 0.0).astype(BF16)
    bcum = sum(_dot(tri_b, part) for part in _split3(g))

    chunks = range(nch)
    for h in range(GLA_HEADS):
        hl = slice(h * GLA_DK, (h + 1) * GLA_DK)
        vl = slice(h * GLA_DV, (h + 1) * GLA_DV)
        q = q_ref[:, hl].astype(F32) * (GLA_DK ** -0.5)
        k = k_ref[:, hl].astype(F32)
        b_h = bcum[:, hl]
        qd, kd, kl, qe, dec = [], [], [], [], []
        for ch in chunks:
            rows = slice(ch * c, (ch + 1) * c)
            b_c = b_h[rows]
            if rev:
                last, mid = b_c[0:1], b_c[c // 2:c // 2 + 1]
            else:
                last, mid = b_c[c - 1:c], b_c[c // 2 - 1:c // 2]
            qd.append(q[rows] * jnp.exp(b_c - mid))
            kd.append(k[rows] * jnp.exp(mid - b_c))
            kl.append((k[rows] * jnp.exp(last - b_c)).astype(BF16))
            qe.append((q[rows] * jnp.exp(b_c)).astype(BF16))
            dec.append(jnp.exp(last))
        qd = jnp.concatenate(qd, axis=0).astype(BF16)
        kd = jnp.concatenate(kd, axis=0).astype(BF16)
        vb = v_ref[:, vl]
        sc = jnp.where(tri, _dot_nt(qd, kd), 0.0)
        o_h = _dot(sc.astype(BF16), vb)
        st = st_ref[h]
        o_parts = [None] * nch
        for ch in (reversed(chunks) if rev else chunks):
            rows = slice(ch * c, (ch + 1) * c)
            o_parts[ch] = o_h[rows] + _dot_nt(qe[ch], st.astype(BF16))
            st = st * dec[ch] + _dot_tn(vb[rows], kl[ch])
        st_ref[h] = st
        o = jnp.concatenate(o_parts, axis=0)
        if rev:
            o = _rms(of_ref[:, vl] + o) * ng_ref[...]
            rg = r_ref[:, vl].astype(F32)
            o = o * (rg * jax.nn.sigmoid(rg))
        o_ref[:, vl] = o.astype(o_ref.dtype)


def _gla(p, wg, gb, rev, of=None, norm_g=None):
    b, s, _ = p.shape
    r = SEQ_BLOCK
    nblk = s // r
    dk = GLA_HEADS * GLA_DK
    dv = GLA_HEADS * GLA_DV
    blk = lambda tb: _seq_block(tb, nblk, rev)
    bt = _first_divisor(b, BATCH_TILE)
    in_specs = [
        pl.BlockSpec((bt, r, dk), lambda bi, tb: (bi, blk(tb), P_GQ // dk)),
        pl.BlockSpec((bt, r, dk), lambda bi, tb: (bi, blk(tb), P_GK // dk)),
        pl.BlockSpec((bt, r, dv), lambda bi, tb: (bi, blk(tb), P_GV // dv)),
        pl.BlockSpec((bt, r, LANES), lambda bi, tb: (bi, blk(tb), P_GZ // LANES)),
        pl.BlockSpec((LANES, dk), lambda bi, tb: (0, 0)),
        pl.BlockSpec((1, dk), lambda bi, tb: (0, 0)),
    ]
    args = [p, p, p, p, wg, gb.reshape(1, dk)]
    if rev:
        in_specs += [
            pl.BlockSpec((bt, r, dv), lambda bi, tb: (bi, blk(tb), 0)),
            pl.BlockSpec((bt, r, dv), lambda bi, tb: (bi, blk(tb), P_GR // dv)),
            pl.BlockSpec((1, GLA_DV), lambda bi, tb: (0, 0)),
        ]
        args += [of, p, norm_g.reshape(1, GLA_DV)]
    return pl.pallas_call(
        functools.partial(_per_sample(_gla_kernel, shared=(4, 5, 8)), rev=rev),
        grid=(b // bt, nblk),
        in_specs=in_specs,
        out_specs=pl.BlockSpec((bt, r, dv), lambda bi, tb: (bi, blk(tb), 0)),
        out_shape=jax.ShapeDtypeStruct((b, s, dv), BF16 if rev else F32),
        scratch_shapes=[pltpu.VMEM((bt, GLA_HEADS, GLA_DV, GLA_DK), F32)],
        compiler_params=_params("arbitrary", "arbitrary"),
        name="gla_rev" if rev else "gla_fwd",
    )(*args)


def _head_perm():
    quarter = HEAD_DIM // 4
    idx = jnp.arange(HEAD_DIM).reshape(4, quarter)
    return jnp.concatenate([idx[0], idx[2], idx[1], idx[3]])


def _rope_tables(n_lat, ctx_len):
    half = HEAD_DIM // 4
    t = jnp.arange(n_lat)
    inv_freq = ROPE_THETA ** (-jnp.arange(0, 2 * half, 2, dtype=F32) / (2 * half))
    ang_r = (t // GRID_W).astype(F32)[:, None] * inv_freq[None, :]
    ang_c = (t % GRID_W).astype(F32)[:, None] * inv_freq[None, :]
    cos = jnp.concatenate([jnp.cos(ang_r), jnp.cos(ang_c)] * 2, axis=-1)
    sin = jnp.concatenate([-jnp.sin(ang_r), -jnp.sin(ang_c), jnp.sin(ang_r), jnp.sin(ang_c)], axis=-1)
    cos = jnp.concatenate([jnp.ones((ctx_len, HEAD_DIM), F32), cos], axis=0)
    sin = jnp.concatenate([jnp.zeros((ctx_len, HEAD_DIM), F32), sin], axis=0)
    return cos, sin


def _qkv_kernel(aq_ref, ak_ref, av_ref, cos_ref, sin_ref, qg_ref, kg_ref, q_ref, k_ref, v_ref):
    tm = aq_ref.shape[0]
    cos, sin = cos_ref[...], sin_ref[...]

    def norm_rope(x, g):
        y = _rms(x.astype(F32)) * g
        return y * cos + pltpu.roll(y, HEAD_DIM // 2, 1) * sin

    for h in range(ATT_HEADS):
        sl = slice(h * HEAD_DIM, (h + 1) * HEAD_DIM)
        q_ref[:, sl] = (norm_rope(aq_ref[:, sl], qg_ref[...]) * Q_SCALE_LOG2).astype(BF16)
    for h in range(ATT_KV_HEADS):
        sl = slice(h * HEAD_DIM, (h + 1) * HEAD_DIM)
        k_ref[:, sl] = norm_rope(ak_ref[:, sl], kg_ref[...]).astype(BF16)
        v_ref[:, 2 * h * HEAD_DIM:(2 * h + 1) * HEAD_DIM] = av_ref[:, sl]
        v_ref[:, (2 * h + 1) * HEAD_DIM:(2 * h + 2) * HEAD_DIM] = jnp.ones((tm, HEAD_DIM), BF16)


def _qkv(p, cos, sin, q_norm_g, k_norm_g):
    b, s, _ = p.shape
    tm = _first_divisor(s, (768, 256))
    dq = ATT_HEADS * HEAD_DIM
    dkv = ATT_KV_HEADS * HEAD_DIM
    return pl.pallas_call(
        _qkv_kernel,
        grid=(b, s // tm),
        in_specs=[
            pl.BlockSpec((None, tm, dq), lambda bi, i: (bi, i, P_AQ // dq)),
            pl.BlockSpec((None, tm, dkv), lambda bi, i: (bi, i, P_AK // dkv)),
            pl.BlockSpec((None, tm, dkv), lambda bi, i: (bi, i, P_AV // dkv)),
            pl.BlockSpec((tm, HEAD_DIM), lambda bi, i: (i, 0)),
            pl.BlockSpec((tm, HEAD_DIM), lambda bi, i: (i, 0)),
            pl.BlockSpec((1, HEAD_DIM), lambda bi, i: (0, 0)),
            pl.BlockSpec((1, HEAD_DIM), lambda bi, i: (0, 0)),
        ],
        out_specs=[
            pl.BlockSpec((None, tm, dq), lambda bi, i: (bi, i, 0)),
            pl.BlockSpec((None, tm, dkv), lambda bi, i: (bi, i, 0)),
            pl.BlockSpec((None, tm, 2 * dkv), lambda bi, i: (bi, i, 0)),
        ],
        out_shape=[
            jax.ShapeDtypeStruct((b, s, dq), BF16),
            jax.ShapeDtypeStruct((b, s, dkv), BF16),
            jax.ShapeDtypeStruct((b, s, 2 * dkv), BF16),
        ],
        compiler_params=_params("parallel", "parallel"),
        name="qkv",
    )(p, p, p, cos, sin, q_norm_g.reshape(1, HEAD_DIM), k_norm_g.reshape(1, HEAD_DIM))


def _attn_scores(qs_ref, k_ref, start, size):
    return _dot_nt(qs_ref[...], k_ref[pl.ds(start, size), :])


def _attn_consume(s, v, m_ref, acc_ref):
    reps = s.shape[1] // LANES
    m_prev = m_ref[...]
    m_new = jnp.maximum(m_prev, jnp.max(s, axis=-1, keepdims=True))
    p = jnp.exp2(s - jnp.concatenate([m_new] * reps, axis=1))
    alpha = jnp.exp2(m_prev - m_new)
    acc_ref[...] = jnp.concatenate([alpha] * 2, axis=1) * acc_ref[...] + _dot(p.astype(BF16), v)
    m_ref[...] = m_new


def _attn_kernel(q_ref, k_ref, v_ref, o_ref, qs_ref, m_ref, acc_ref, sa_ref, sb_ref, *, ctx_len, ck):
    tq = q_ref.shape[0]
    nck = k_ref.shape[0] // ck
    is_ctx_tile = pl.program_id(2) == 0
    for h in range(ATT_GROUP):
        qs_ref[h * tq:(h + 1) * tq, :] = q_ref[:, h * HEAD_DIM:(h + 1) * HEAD_DIM]
    m_ref[...] = jnp.full_like(m_ref, -jnp.inf)
    acc_ref[...] = jnp.zeros_like(acc_ref)

    @pl.when(is_ctx_tile)
    def _():
        _attn_consume(_attn_scores(qs_ref, k_ref, 0, ctx_len), v_ref[0:ctx_len, :], m_ref, acc_ref)

    @pl.when(jnp.logical_not(is_ctx_tile))
    def _():
        def chunk(c):
            return pl.multiple_of(c * ck, LANES)

        sa_ref[...] = _attn_scores(qs_ref, k_ref, 0, ck)

        def body(i, carry):
            sb_ref[...] = _attn_scores(qs_ref, k_ref, chunk(2 * i + 1), ck)
            _attn_consume(sa_ref[...], v_ref[pl.ds(chunk(2 * i), ck), :], m_ref, acc_ref)
            sa_ref[...] = _attn_scores(qs_ref, k_ref, chunk(2 * i + 2), ck)
            _attn_consume(sb_ref[...], v_ref[pl.ds(chunk(2 * i + 1), ck), :], m_ref, acc_ref)
            return carry

        pairs = (nck - 1) // 2
        lax.fori_loop(0, pairs, body, 0)
        if nck - 1 - 2 * pairs:
            sb_ref[...] = _attn_scores(qs_ref, k_ref, (nck - 1) * ck, ck)
            _attn_consume(sa_ref[...], v_ref[(nck - 2) * ck:(nck - 1) * ck, :], m_ref, acc_ref)
            _attn_consume(sb_ref[...], v_ref[(nck - 1) * ck:nck * ck, :], m_ref, acc_ref)
        else:
            _attn_consume(sa_ref[...], v_ref[(nck - 1) * ck:nck * ck, :], m_ref, acc_ref)

    acc = acc_ref[...]
    out = acc[:, :HEAD_DIM] / acc[:, HEAD_DIM:]
    for h in range(ATT_GROUP):
        o_ref[:, h * HEAD_DIM:(h + 1) * HEAD_DIM] = out[h * tq:(h + 1) * tq].astype(o_ref.dtype)


def _attn(qn, kn, vn, ctx_len):
    b, s, dq = qn.shape
    tq = SEQ_BLOCK
    ck = _first_divisor(s, (768, 640, 256))
    gw = ATT_GROUP * HEAD_DIM
    rows = ATT_GROUP * tq
    return pl.pallas_call(
        functools.partial(_attn_kernel, ctx_len=ctx_len, ck=ck),
        grid=(b, ATT_KV_HEADS, s // tq),
        in_specs=[
            pl.BlockSpec((None, tq, gw), lambda bi, h, i: (bi, i, h)),
            pl.BlockSpec((None, s, HEAD_DIM), lambda bi, h, i: (bi, 0, h)),
            pl.BlockSpec((None, s, 2 * HEAD_DIM), lambda bi, h, i: (bi, 0, h)),
        ],
        out_specs=pl.BlockSpec((None, tq, gw), lambda bi, h, i: (bi, i, h)),
        out_shape=jax.ShapeDtypeStruct((b, s, dq), BF16),
        scratch_shapes=[
            pltpu.VMEM((rows, HEAD_DIM), BF16),
            pltpu.VMEM((rows, LANES), F32),
            pltpu.VMEM((rows, 2 * HEAD_DIM), F32),
            pltpu.VMEM((rows, ck), F32),
            pltpu.VMEM((rows, ck), F32),
        ],
        compiler_params=_params("parallel", "parallel", "arbitrary"),
        name="attn",
    )(qn, kn, vn)


def _merge_kernel(x_ref, y1_ref, y2_ref, y3_ref, bg_ref, modb_ref, modc_ref, wb_ref, wo_ref, o_ref,
                  *, tm, ctx_len):
    d = x_ref.shape[1]
    gate = jax.nn.sigmoid(bg_ref[...].astype(F32))
    m = (gate[:, 0:d] * _dot(y1_ref[...], wb_ref[0])
         + gate[:, d:2 * d] * _dot(y2_ref[...], wb_ref[1])
         + gate[:, 2 * d:3 * d] * _dot(y3_ref[...], wb_ref[2]))
    upd = _dot(m.astype(BF16), wo_ref[...])
    is_ctx = _ctx_rows(pl.program_id(1), tm, ctx_len)
    ga = jnp.where(is_ctx, modc_ref[2:3, :], modb_ref[2:3, :])
    o_ref[...] = x_ref[...] + ga * upd


def _merge(xs, y1, y2, y3, p, mods, wb, wo, ctx_len):
    b, s, d = xs.shape
    tm = _first_divisor(s, (768, 640, 256))
    nb = mods.shape[0] - 1
    row = lambda bi, i: (bi, i, 0)
    return pl.pallas_call(
        functools.partial(_merge_kernel, tm=tm, ctx_len=ctx_len),
        grid=(b, s // tm),
        in_specs=[
            pl.BlockSpec((None, tm, d), row),
            pl.BlockSpec((None, tm, d), row),
            pl.BlockSpec((None, tm, d), row),
            pl.BlockSpec((None, tm, d), row),
            pl.BlockSpec((None, tm, N_BRANCHES * d), lambda bi, i: (bi, i, P_BG // (N_BRANCHES * d))),
            pl.BlockSpec((None, 6, d), lambda bi, i: (bi, 0, 0)),
            pl.BlockSpec((None, 6, d), lambda bi, i: (nb, 0, 0)),
            pl.BlockSpec((N_BRANCHES, d, d), lambda bi, i: (0, 0, 0)),
            pl.BlockSpec((d, d), lambda bi, i: (0, 0)),
        ],
        out_specs=pl.BlockSpec((None, tm, d), row),
        out_shape=jax.ShapeDtypeStruct((b, s, d), F32),
        compiler_params=_params("parallel", "parallel"),
        name="merge",
    )(xs, y1, y2, y3, p, mods, mods, wb, wo)


def _mlp_kernel(x_ref, modb_ref, modc_ref, g_ref, w1_ref, w2_ref, o_ref, h_ref, acc_ref, *, tm, ctx_len):
    i = pl.program_id(1)
    j = pl.program_id(2)
    is_ctx = _ctx_rows(i, tm, ctx_len)

    @pl.when(j == 0)
    def _():
        y = _rms(x_ref[...]) * g_ref[...]
        shift = jnp.where(is_ctx, modc_ref[3:4, :], modb_ref[3:4, :])
        scale = jnp.where(is_ctx, modc_ref[4:5, :], modb_ref[4:5, :])
        h_ref[...] = (y * (1.0 + scale) + shift).astype(BF16)
        acc_ref[...] = jnp.zeros_like(acc_ref)

    t = jnp.maximum(_dot(h_ref[...], w1_ref[...]), 0.0)
    acc_ref[...] += _dot((t * t).astype(BF16), w2_ref[...])

    @pl.when(j == pl.num_programs(2) - 1)
    def _():
        ga = jnp.where(is_ctx, modc_ref[5:6, :], modb_ref[5:6, :])
        o_ref[...] = x_ref[...] + ga * acc_ref[...]


def _mlp(xs, mods, norm_g, w1, w2, ctx_len):
    b, s, d = xs.shape
    dff = w1.shape[1]
    tm = _first_divisor(s, (1408, 1280, 640, 256))
    tf = _first_divisor(dff, (1024, 512))
    nb = mods.shape[0] - 1
    return pl.pallas_call(
        functools.partial(_mlp_kernel, tm=tm, ctx_len=ctx_len),
        grid=(b, s // tm, dff // tf),
        in_specs=[
            pl.BlockSpec((None, tm, d), lambda bi, i, j: (bi, i, 0)),
            pl.BlockSpec((None, 6, d), lambda bi, i, j: (bi, 0, 0)),
            pl.BlockSpec((None, 6, d), lambda bi, i, j: (nb, 0, 0)),
            pl.BlockSpec((1, d), lambda bi, i, j: (0, 0)),
            pl.BlockSpec((d, tf), lambda bi, i, j: (0, j)),
            pl.BlockSpec((tf, d), lambda bi, i, j: (j, 0)),
        ],
        out_specs=pl.BlockSpec((None, tm, d), lambda bi, i, j: (bi, i, 0)),
        out_shape=jax.ShapeDtypeStruct((b, s, d), F32),
        scratch_shapes=[pltpu.VMEM((tm, d), BF16), pltpu.VMEM((tm, d), F32)],
        compiler_params=_params("parallel", "parallel", "arbitrary"),
        name="mlp",
    )(xs, mods, mods, norm_g.reshape(1, d), w1, w2)


def _pack_w_in(w_in):
    lx, ly, gq, gk, gv, gr, gz, aq, ak, av, bg = jnp.split(
        w_in, (1024, 2048, 2560, 3072, 4096, 5120, 5152, 6176, 6432, 6688), axis=-1)
    pad = jnp.zeros(w_in.shape[:-1] + (P_GZ_PAD - gz.shape[-1],), w_in.dtype)
    perm = _head_perm()

    def permute_heads(w):
        heads = w.reshape(w.shape[:-1] + (w.shape[-1] // HEAD_DIM, HEAD_DIM))
        return heads[..., perm].reshape(w.shape)

    return jnp.concatenate([lx, ly, gv, gr, permute_heads(aq), gq, gk, bg, permute_heads(ak), av, gz, pad],
                           axis=-1).astype(BF16)


def _block_diag(w):
    per = MXU_DIM // LRU_BLOCK_W
    lead = w.shape[:-3]
    wg = w.reshape(lead + (w.shape[-3] // per, per, LRU_BLOCK_W, LRU_BLOCK_W))
    eye = jnp.eye(per, dtype=w.dtype)
    bd = jnp.einsum("...gaij,ac->...gaicj", wg, eye)
    return bd.reshape(lead + (w.shape[-3] // per, MXU_DIM, MXU_DIM)).astype(BF16)


def _pad_gate_w(gate_w):
    n_layers, n_dir, rank, dk = gate_w.shape
    out = jnp.zeros((n_layers, n_dir, LANES, dk), gate_w.dtype)
    for d in range(n_dir):
        out = out.at[:, d, d * rank:(d + 1) * rank].set(gate_w[:, d])
    return out.astype(BF16)


def kernel(x, c, ctx, c_ctx, mod_w, mod_b, norm1_g, norm2_g, w_in, conv_w, conv_b, lru_a_w, lru_a_b,
           lru_x_w, lru_x_b, lru_lambda, gla_gate_w, gla_gate_b, gla_norm_g, q_norm_g, k_norm_g,
           w_branch, w_out, mlp_w1, mlp_w2):
    n_lat = x.shape[1]
    ctx_len = ctx.shape[1]
    assert x.shape[2] == D and ctx_len == SEQ_BLOCK and n_lat % SEQ_BLOCK == 0
    n_layers = mod_w.shape[0]

    xs = jnp.concatenate([ctx, x], axis=1)
    mods = _modulation(c, c_ctx, mod_w, mod_b)
    cos, sin = _rope_tables(n_lat, ctx_len)
    perm = _head_perm()
    wp = _pack_w_in(w_in)
    wa = _block_diag(lru_a_w)
    wx = _block_diag(lru_x_w)
    wg = _pad_gate_w(gla_gate_w)
    wb = w_branch.astype(BF16)
    wo = w_out.astype(BF16)
    w1 = mlp_w1.astype(BF16)
    w2 = mlp_w2.astype(BF16)

    for l in range(n_layers):
        p = _proj(xs, mods[l], norm1_g[l], wp[l], ctx_len)
        lru = lambda dr, **kw: _lru(p, conv_w[l], conv_b[l], wa[l, dr], lru_a_b[l, dr], wx[l, dr],
                                    lru_x_b[l, dr], lru_lambda[l, dr], rev=bool(dr), **kw)
        y1 = lru(1, hf=lru(0))
        gla = lambda dr, **kw: _gla(p, wg[l, dr], gla_gate_b[l, dr], rev=bool(dr), **kw)
        y2 = gla(1, of=gla(0), norm_g=gla_norm_g[l])
        qn, kn, vn = _qkv(p, cos, sin, q_norm_g[l][perm], k_norm_g[l][perm])
        y3 = _attn(qn, kn, vn, ctx_len)
        x1 = _merge(xs, y1, y2, y3, p, mods[l], wb[l], wo[l], ctx_len)
        xs = _mlp(x1, mods[l], norm2_g[l], w1[l], w2[l], ctx_len)
    return xs[:, ctx_len:]
```

```python
import functools

import jax
import jax.numpy as jnp
from jax import lax
from jax.experimental import pallas as pl
from jax.experimental.pallas import tpu as pltpu

F32 = jnp.float32
BF16 = jnp.bfloat16

NORM_EPS = 1e-6
F32_TINY = 1e-37
GRID_W = 64
ROPE_THETA = 10000.0
RGLRU_C = 8.0
LRU_BLOCK_W = 64
CONV_WIDTH = 4
GLA_HEADS = 4
GLA_DK = 128
GLA_DV = 256
GLA_GATE_RANK = 16
GLA_TAU = 16.0
GLA_CHUNK = 64
ATT_HEADS = 8
ATT_KV_HEADS = 2
ATT_GROUP = ATT_HEADS // ATT_KV_HEADS
HEAD_DIM = 128
N_BRANCHES = 3
Q_SCALE_LOG2 = HEAD_DIM ** -0.5 * 1.4426950408889634
ATTN_SAFE_SHIFT = 40.0

LANES = 128
SUBLANES = 8
MXU_DIM = 256
VMEM_LIMIT_BYTES = 56 * 1024 * 1024

SEQ_BLOCK = 256
HALO = 2 * SUBLANES

D = 1024
P_LX, P_LY, P_GV, P_GR, P_AQ = 0, 1024, 2048, 3072, 4096
P_GQ, P_GK, P_BG = 5120, 5632, 6144
P_AK, P_AV, P_GZ = 9216, 9472, 9728
P_GZ_PAD = 256
P_WIDTH = P_GZ + P_GZ_PAD
PROJ_TN = 768


def _first_divisor(n, candidates):
    for c in candidates:
        if n % c == 0:
            return c
    raise ValueError(f"no tile in {candidates} divides {n}")


def _params(*sem):
    return pltpu.CompilerParams(dimension_semantics=sem, vmem_limit_bytes=VMEM_LIMIT_BYTES)


def _rms(x):
    return x * lax.rsqrt(jnp.mean(x * x, axis=-1, keepdims=True) + NORM_EPS)


def _dot(a, b):
    return jnp.dot(a, b, preferred_element_type=F32)


def _dot_nt(a, b):
    return lax.dot_general(a, b, (((1,), (1,)), ((), ())), preferred_element_type=F32)


def _dot_tn(a, b):
    return lax.dot_general(a, b, (((0,), (0,)), ((), ())), preferred_element_type=F32)


def _ctx_rows(tile_idx, tm, ctx_len):
    rows = tile_idx * tm + lax.broadcasted_iota(jnp.int32, (tm, 1), 0)
    return rows < ctx_len


def _mod_kernel(c_ref, w_ref, b_ref, o_ref):
    c = c_ref[...]
    cs = (c * jax.nn.sigmoid(c)).astype(BF16)
    o_ref[...] = _dot(cs, w_ref[...].astype(BF16)) + b_ref[...]


def _modulation(c, c_ctx, mod_w, mod_b):
    n_layers, d, n6 = mod_w.shape
    b = c.shape[0]
    rows = -(-(b + 1) // SUBLANES) * SUBLANES
    cs = jnp.zeros((rows, d), F32).at[:b].set(c).at[b].set(c_ctx)
    tn = _first_divisor(n6, (1536, 1024, 512))
    out = pl.pallas_call(
        _mod_kernel,
        grid=(n_layers, n6 // tn),
        in_specs=[
            pl.BlockSpec((rows, d), lambda l, j: (0, 0)),
            pl.BlockSpec((None, d, tn), lambda l, j: (l, 0, j)),
            pl.BlockSpec((None, 1, tn), lambda l, j: (l, 0, j)),
        ],
        out_specs=pl.BlockSpec((None, rows, tn), lambda l, j: (l, 0, j)),
        out_shape=jax.ShapeDtypeStruct((n_layers, rows, n6), F32),
        compiler_params=_params("parallel", "parallel"),
        name="modulation",
    )(cs, mod_w, mod_b.reshape(n_layers, 1, n6))
    return out.reshape(n_layers, rows, 6, d)[:, :b + 1]


def _proj_kernel(x_ref, modb_ref, modc_ref, g_ref, w_ref, o_ref, h_ref, *, tm, ctx_len):
    i = pl.program_id(1)

    @pl.when(pl.program_id(2) == 0)
    def _():
        y = _rms(x_ref[...]) * g_ref[...]
        is_ctx = _ctx_rows(i, tm, ctx_len)
        shift = jnp.where(is_ctx, modc_ref[0:1, :], modb_ref[0:1, :])
        scale = jnp.where(is_ctx, modc_ref[1:2, :], modb_ref[1:2, :])
        h_ref[...] = (y * (1.0 + scale) + shift).astype(BF16)

    o_ref[...] = _dot(h_ref[...], w_ref[...]).astype(o_ref.dtype)


def _proj(xs, mods, norm_g, wp, ctx_len):
    b, s, d = xs.shape
    tm = _first_divisor(s, (1408, 1280, 640, 256))
    nb = mods.shape[0] - 1
    return pl.pallas_call(
        functools.partial(_proj_kernel, tm=tm, ctx_len=ctx_len),
        grid=(b, s // tm, P_WIDTH // PROJ_TN),
        in_specs=[
            pl.BlockSpec((None, tm, d), lambda bi, i, j: (bi, i, 0)),
            pl.BlockSpec((None, 6, d), lambda bi, i, j: (bi, 0, 0)),
            pl.BlockSpec((None, 6, d), lambda bi, i, j: (nb, 0, 0)),
            pl.BlockSpec((1, d), lambda bi, i, j: (0, 0)),
            pl.BlockSpec((d, PROJ_TN), lambda bi, i, j: (0, j)),
        ],
        out_specs=pl.BlockSpec((None, tm, PROJ_TN), lambda bi, i, j: (bi, i, j)),
        out_shape=jax.ShapeDtypeStruct((b, s, P_WIDTH), BF16),
        scratch_shapes=[pltpu.VMEM((tm, d), BF16)],
        compiler_params=_params("parallel", "parallel", "arbitrary"),
        name="proj",
    )(xs, mods, mods, norm_g.reshape(1, d), wp)


def _seq_block(tb, nblk, rev):
    if not rev:
        return tb
    return jnp.where(tb == 0, 0, nblk - tb)


def _per_sample(block_fn, shared):
    def body(*refs, **kw):
        for bb in range(refs[0].shape[0]):
            block_fn(*[r if i in shared else r.at[bb] for i, r in enumerate(refs)], **kw)
    return body


BATCH_TILE = (2, 1)


def _gelu_tanh(x):
    return 0.5 * x * (1.0 + jnp.tanh(0.7978845608028654 * (x + 0.044715 * (x * x * x))))


def _segment_perm(r, inverse):
    seg = r // SUBLANES
    ri = lax.broadcasted_iota(jnp.int32, (r, r), 0)
    ci = lax.broadcasted_iota(jnp.int32, (r, r), 1)
    if inverse:
        ri, ci = ci, ri
    src = (ri & (SUBLANES - 1)) * seg + lax.shift_right_logical(ri, 3)
    return jnp.where(ci == src, 1.0, 0.0).astype(BF16)


def _sublane_scan(a, u, carry, sub, rev):
    for k in (1, 2, 4):
        if rev:
            shift, valid = SUBLANES - k, sub < SUBLANES - k
        else:
            shift, valid = k, sub >= k
        a_sh = pltpu.roll(a, shift, 0)
        u_sh = pltpu.roll(u, shift, 0)
        u = u + a * jnp.where(valid, u_sh, 0.0)
        a = a * jnp.where(valid, a_sh, 1.0)
    return a * carry + u


def _lru_kernel(*refs, rev, nblk):
    if rev:
        (lx_ref, prev_ref, next_ref, cw_ref, cb_ref, wa_ref, ab_ref, wx_ref, xb_ref, lam_ref,
         hf_ref, ly_ref, o_ref, carry_ref) = refs
    else:
        (lx_ref, prev_ref, next_ref, cw_ref, cb_ref, wa_ref, ab_ref, wx_ref, xb_ref, lam_ref,
         o_ref, carry_ref) = refs
    tb = pl.program_id(1)
    te = _seq_block(tb, nblk, rev)
    r, w = lx_ref.shape
    seg = r // SUBLANES
    top = SUBLANES - 1

    @pl.when(tb == 0)
    def _():
        carry_ref[...] = jnp.zeros_like(carry_ref)

    perm = _segment_perm(r, inverse=False)
    x = _dot(perm, lx_ref[...])
    sub = lax.broadcasted_iota(jnp.int32, (SUBLANES, w), 0)

    has_prev = te >= 2
    has_next = jnp.logical_and(te >= 1, te <= nblk - 2)
    prev_rows = jnp.where(has_prev, prev_ref[...].astype(F32), 0.0)
    next_rows = jnp.where(has_next, next_ref[...].astype(F32), 0.0)
    edge_m1 = jnp.where(sub == 0, prev_rows[HALO - 1:HALO], pltpu.roll(x[r - SUBLANES:r], 1, 0))
    edge_p1 = jnp.where(sub == top, next_rows[0:1], pltpu.roll(x[0:SUBLANES], top, 0))
    edge_p2 = jnp.where(sub == top, next_rows[1:2], pltpu.roll(x[SUBLANES:2 * SUBLANES], top, 0))
    x_m1 = jnp.concatenate([edge_m1, x[:r - SUBLANES]], axis=0)
    x_p1 = jnp.concatenate([x[SUBLANES:], edge_p1], axis=0)
    x_p2 = jnp.concatenate([x[2 * SUBLANES:], edge_p1, edge_p2], axis=0)
    xc = (cw_ref[0:1, :] * x_m1 + cw_ref[1:2, :] * x + cw_ref[2:3, :] * x_p1 + cw_ref[3:4, :] * x_p2
          + cb_ref[...])

    xcb = xc.astype(BF16)
    lam = lam_ref[...]
    log_a_scale = -RGLRU_C * (jnp.maximum(-lam, 0.0) + jnp.log1p(jnp.exp(-jnp.abs(lam))))
    a_parts, u_parts = [], []
    for gi in range(w // MXU_DIM):
        sl = slice(gi * MXU_DIM, (gi + 1) * MXU_DIM)
        rg = jax.nn.sigmoid(_dot(xcb[:, sl], wa_ref[gi]) + ab_ref[:, sl])
        ig = jax.nn.sigmoid(_dot(xcb[:, sl], wx_ref[gi]) + xb_ref[:, sl])
        log_a = rg * log_a_scale[:, sl]
        a_g = jnp.exp(log_a)
        a_parts.append(a_g)
        one_minus_a2 = -jnp.tanh(log_a) * (1.0 + a_g * a_g)
        root = one_minus_a2 * lax.rsqrt(jnp.maximum(one_minus_a2, F32_TINY))
        u_parts.append(root * (ig * xc[:, sl]))
    a = jnp.concatenate(a_parts, axis=1)
    u = jnp.concatenate(u_parts, axis=1)

    steps = range(seg)
    h_loc = [None] * seg
    p_cum = [None] * seg
    hl = pc = None
    for i in (reversed(steps) if rev else steps):
        rows = slice(i * SUBLANES, (i + 1) * SUBLANES)
        hl = u[rows] if hl is None else a[rows] * hl + u[rows]
        pc = a[rows] if pc is None else a[rows] * pc
        h_loc[i], p_cum[i] = hl, pc
    carry = carry_ref[...]
    h_edge = _sublane_scan(pc, hl, carry, sub, rev)
    if rev:
        seg_in = jnp.where(sub == top, carry, pltpu.roll(h_edge, top, 0))
        carry_ref[...] = h_edge[0:1, :]
    else:
        seg_in = jnp.where(sub == 0, carry, pltpu.roll(h_edge, 1, 0))
        carry_ref[...] = h_edge[top:SUBLANES, :]
    h_all = jnp.concatenate([h_loc[i] + p_cum[i] * seg_in for i in steps], axis=0)
    if rev:
        gate = _gelu_tanh(_dot(perm, ly_ref[...]))
        y = ((hf_ref[...] + h_all) * gate).astype(BF16)
        o_ref[...] = _dot(_segment_perm(r, inverse=True), y).astype(o_ref.dtype)
    else:
        o_ref[...] = h_all


def _lru(p, conv_w, conv_b, wa, ab, wx, xb, lam, rev, hf=None):
    b, s, _ = p.shape
    w = conv_w.shape[1]
    r = SEQ_BLOCK
    nblk = s // r
    per_halo = r // HALO
    last_halo = s // HALO - 1
    blk = lambda tb: _seq_block(tb, nblk, rev)
    ng = w // MXU_DIM
    bt = _first_divisor(b, BATCH_TILE)
    in_specs = [
        pl.BlockSpec((bt, r, w), lambda bi, tb: (bi, blk(tb), P_LX // w)),
        pl.BlockSpec((bt, HALO, w), lambda bi, tb: (bi, jnp.maximum(blk(tb) * per_halo - 1, 0), P_LX // w)),
        pl.BlockSpec((bt, HALO, w),
                     lambda bi, tb: (bi, jnp.minimum((blk(tb) + 1) * per_halo, last_halo), P_LX // w)),
        pl.BlockSpec((CONV_WIDTH, w), lambda bi, tb: (0, 0)),
        pl.BlockSpec((1, w), lambda bi, tb: (0, 0)),
        pl.BlockSpec((ng, MXU_DIM, MXU_DIM), lambda bi, tb: (0, 0, 0)),
        pl.BlockSpec((1, w), lambda bi, tb: (0, 0)),
        pl.BlockSpec((ng, MXU_DIM, MXU_DIM), lambda bi, tb: (0, 0, 0)),
        pl.BlockSpec((1, w), lambda bi, tb: (0, 0)),
        pl.BlockSpec((1, w), lambda bi, tb: (0, 0)),
    ]
    args = [p, p, p, conv_w, conv_b.reshape(1, w), wa, ab.reshape(1, w), wx, xb.reshape(1, w),
            lam.reshape(1, w)]
    if rev:
        in_specs += [
            pl.BlockSpec((bt, r, w), lambda bi, tb: (bi, blk(tb), 0)),
            pl.BlockSpec((bt, r, w), lambda bi, tb: (bi, blk(tb), P_LY // w)),
        ]
        args += [hf, p]
    return pl.pallas_call(
        functools.partial(_per_sample(_lru_kernel, shared=range(3, 10)), rev=rev, nblk=nblk),
        grid=(b // bt, nblk),
        in_specs=in_specs,
        out_specs=pl.BlockSpec((bt, r, w), lambda bi, tb: (bi, blk(tb), 0)),
        out_shape=jax.ShapeDtypeStruct((b, s, w), BF16 if rev else F32),
        scratch_shapes=[pltpu.VMEM((bt, 1, w), F32)],
        compiler_params=_params("arbitrary", "arbitrary"),
        name="lru_rev" if rev else "lru_fwd",
    )(*args)


def _split3(x):
    hi = x.astype(BF16)
    r1 = x - hi.astype(F32)
    mid = r1.astype(BF16)
    lo = (r1 - mid.astype(F32)).astype(BF16)
    return hi, mid, lo


def _gla_kernel(*refs, rev):
    if rev:
        q_ref, k_ref, v_ref, z_ref, wg_ref, gb_ref, of_ref, r_ref, ng_ref, o_ref, st_ref = refs
    else:
        q_ref, k_ref, v_ref, z_ref, wg_ref, gb_ref, o_ref, st_ref = refs
    r = q_ref.shape[0]
    c = GLA_CHUNK
    nch = r // c

    @pl.when(pl.program_id(1) == 0)
    def _():
        st_ref[...] = jnp.zeros_like(st_ref)

    zz = _dot(z_ref[...], wg_ref[...]) + gb_ref[...]
    g = (jnp.minimum(zz, 0.0) - jnp.log1p(jnp.exp(-jnp.abs(zz)))) * (1.0 / GLA_TAU)

    ri = lax.broadcasted_iota(jnp.int32, (r, r), 0)
    ci = lax.broadcasted_iota(jnp.int32, (r, r), 1)
    same = lax.shift_right_logical(ri, 6) == lax.shift_right_logical(ci, 6)
    tri = jnp.logical_and(same, (ci >= ri) if rev else (ci <= ri))
    tri_b = jnp.where(tri, 1.0, 0.0).astype(BF16)
    bcum = sum(_dot(tri_b, part) for part in _split3(g))

    chunks = range(nch)
    for h in range(GLA_HEADS):
        hl = slice(h * GLA_DK, (h + 1) * GLA_DK)
        vl = slice(h * GLA_DV, (h + 1) * GLA_DV)
        q = q_ref[:, hl].astype(F32) * (GLA_DK ** -0.5)
        k = k_ref[:, hl].astype(F32)
        b_h = bcum[:, hl]
        qd, kd, kl, qe, dec = [], [], [], [], []
        for ch in chunks:
            rows = slice(ch * c, (ch + 1) * c)
            b_c = b_h[rows]
            if rev:
                last, mid = b_c[0:1], b_c[c // 2:c // 2 + 1]
            else:
                last, mid = b_c[c - 1:c], b_c[c // 2 - 1:c // 2]
            qd.append(q[rows] * jnp.exp(b_c - mid))
            kd.append(k[rows] * jnp.exp(mid - b_c))
            kl.append((k[rows] * jnp.exp(last - b_c)).astype(BF16))
            qe.append((q[rows] * jnp.exp(b_c)).astype(BF16))
            dec.append(jnp.exp(last))
        qd = jnp.concatenate(qd, axis=0).astype(BF16)
        kd = jnp.concatenate(kd, axis=0).astype(BF16)
        vb = v_ref[:, vl]
        sc = jnp.where(tri, _dot_nt(qd, kd), 0.0)
        o_h = _dot(sc.astype(BF16), vb)
        st = st_ref[h]
        o_parts = [None] * nch
        for ch in (reversed(chunks) if rev else chunks):
            rows = slice(ch * c, (ch + 1) * c)
            o_parts[ch] = o_h[rows] + _dot_nt(qe[ch], st.astype(BF16))
            st = st * dec[ch] + _dot_tn(vb[rows], kl[ch])
        st_ref[h] = st
        o = jnp.concatenate(o_parts, axis=0)
        if rev:
            o = _rms(of_ref[:, vl] + o) * ng_ref[...]
            rg = r_ref[:, vl].astype(F32)
            o = o * (rg * jax.nn.sigmoid(rg))
        o_ref[:, vl] = o.astype(o_ref.dtype)


def _gla(p, wg, gb, rev, of=None, norm_g=None):
    b, s, _ = p.shape
    r = SEQ_BLOCK
    nblk = s // r
    dk = GLA_HEADS * GLA_DK
    dv = GLA_HEADS * GLA_DV
    blk = lambda tb: _seq_block(tb, nblk, rev)
    bt = _first_divisor(b, BATCH_TILE)
    in_specs = [
        pl.BlockSpec((bt, r, dk), lambda bi, tb: (bi, blk(tb), P_GQ // dk)),
        pl.BlockSpec((bt, r, dk), lambda bi, tb: (bi, blk(tb), P_GK // dk)),
        pl.BlockSpec((bt, r, dv), lambda bi, tb: (bi, blk(tb), P_GV // dv)),
        pl.BlockSpec((bt, r, LANES), lambda bi, tb: (bi, blk(tb), P_GZ // LANES)),
        pl.BlockSpec((LANES, dk), lambda bi, tb: (0, 0)),
        pl.BlockSpec((1, dk), lambda bi, tb: (0, 0)),
    ]
    args = [p, p, p, p, wg, gb.reshape(1, dk)]
    if rev:
        in_specs += [
            pl.BlockSpec((bt, r, dv), lambda bi, tb: (bi, blk(tb), 0)),
            pl.BlockSpec((bt, r, dv), lambda bi, tb: (bi, blk(tb), P_GR // dv)),
            pl.BlockSpec((1, GLA_DV), lambda bi, tb: (0, 0)),
        ]
        args += [of, p, norm_g.reshape(1, GLA_DV)]
    return pl.pallas_call(
        functools.partial(_per_sample(_gla_kernel, shared=(4, 5, 8)), rev=rev),
        grid=(b // bt, nblk),
        in_specs=in_specs,
        out_specs=pl.BlockSpec((bt, r, dv), lambda bi, tb: (bi, blk(tb), 0)),
        out_shape=jax.ShapeDtypeStruct((b, s, dv), BF16 if rev else F32),
        scratch_shapes=[pltpu.VMEM((bt, GLA_HEADS, GLA_DV, GLA_DK), F32)],
        compiler_params=_params("arbitrary", "arbitrary"),
        name="gla_rev" if rev else "gla_fwd",
    )(*args)


def _head_perm():
    quarter = HEAD_DIM // 4
    idx = jnp.arange(HEAD_DIM).reshape(4, quarter)
    return jnp.concatenate([idx[0], idx[2], idx[1], idx[3]])


def _rope_tables(n_lat, ctx_len):
    half = HEAD_DIM // 4
    t = jnp.arange(n_lat)
    inv_freq = ROPE_THETA ** (-jnp.arange(0, 2 * half, 2, dtype=F32) / (2 * half))
    ang_r = (t // GRID_W).astype(F32)[:, None] * inv_freq[None, :]
    ang_c = (t % GRID_W).astype(F32)[:, None] * inv_freq[None, :]
    cos = jnp.concatenate([jnp.cos(ang_r), jnp.cos(ang_c)] * 2, axis=-1)
    sin = jnp.concatenate([-jnp.sin(ang_r), -jnp.sin(ang_c), jnp.sin(ang_r), jnp.sin(ang_c)], axis=-1)
    cos = jnp.concatenate([jnp.ones((ctx_len, HEAD_DIM), F32), cos], axis=0)
    sin = jnp.concatenate([jnp.zeros((ctx_len, HEAD_DIM), F32), sin], axis=0)
    return cos, sin


def _qkv_kernel(aq_ref, ak_ref, av_ref, cos_ref, sin_ref, qg_ref, kg_ref, q_ref, k_ref, v_ref):
    tm = aq_ref.shape[0]
    cos, sin = cos_ref[...], sin_ref[...]

    def norm_rope(x, g):
        y = _rms(x.astype(F32)) * g
        return y * cos + pltpu.roll(y, HEAD_DIM // 2, 1) * sin

    for h in range(ATT_HEADS):
        sl = slice(h * HEAD_DIM, (h + 1) * HEAD_DIM)
        q_ref[:, sl] = (norm_rope(aq_ref[:, sl], qg_ref[...]) * Q_SCALE_LOG2).astype(BF16)
    for h in range(ATT_KV_HEADS):
        sl = slice(h * HEAD_DIM, (h + 1) * HEAD_DIM)
        k_ref[:, sl] = norm_rope(ak_ref[:, sl], kg_ref[...]).astype(BF16)
        v_ref[:, 2 * h * HEAD_DIM:(2 * h + 1) * HEAD_DIM] = av_ref[:, sl]
        v_ref[:, (2 * h + 1) * HEAD_DIM:(2 * h + 2) * HEAD_DIM] = jnp.ones((tm, HEAD_DIM), BF16)


def _qkv(p, cos, sin, q_norm_g, k_norm_g):
    b, s, _ = p.shape
    tm = _first_divisor(s, (768, 256))
    dq = ATT_HEADS * HEAD_DIM
    dkv = ATT_KV_HEADS * HEAD_DIM
    return pl.pallas_call(
        _qkv_kernel,
        grid=(b, s // tm),
        in_specs=[
            pl.BlockSpec((None, tm, dq), lambda bi, i: (bi, i, P_AQ // dq)),
            pl.BlockSpec((None, tm, dkv), lambda bi, i: (bi, i, P_AK // dkv)),
            pl.BlockSpec((None, tm, dkv), lambda bi, i: (bi, i, P_AV // dkv)),
            pl.BlockSpec((tm, HEAD_DIM), lambda bi, i: (i, 0)),
            pl.BlockSpec((tm, HEAD_DIM), lambda bi, i: (i, 0)),
            pl.BlockSpec((1, HEAD_DIM), lambda bi, i: (0, 0)),
            pl.BlockSpec((1, HEAD_DIM), lambda bi, i: (0, 0)),
        ],
        out_specs=[
            pl.BlockSpec((None, tm, dq), lambda bi, i: (bi, i, 0)),
            pl.BlockSpec((None, tm, dkv), lambda bi, i: (bi, i, 0)),
            pl.BlockSpec((None, tm, 2 * dkv), lambda bi, i: (bi, i, 0)),
        ],
        out_shape=[
            jax.ShapeDtypeStruct((b, s, dq), BF16),
            jax.ShapeDtypeStruct((b, s, dkv), BF16),
            jax.ShapeDtypeStruct((b, s, 2 * dkv), BF16),
        ],
        compiler_params=_params("parallel", "parallel"),
        name="qkv",
    )(p, p, p, cos, sin, q_norm_g.reshape(1, HEAD_DIM), k_norm_g.reshape(1, HEAD_DIM))


def _attn_scores(qs_ref, k_ref, start, size):
    return _dot_nt(qs_ref[...], k_ref[pl.ds(start, size), :])


def _attn_consume(s, v, m_ref, acc_ref):
    reps = s.shape[1] // LANES
    m_prev = m_ref[...]
    m_new = jnp.maximum(m_prev, jnp.max(s, axis=-1, keepdims=True))
    p = jnp.exp2(s - jnp.concatenate([m_new] * reps, axis=1))
    alpha = jnp.exp2(m_prev - m_new)
    acc_ref[...] = jnp.concatenate([alpha] * 2, axis=1) * acc_ref[...] + _dot(p.astype(BF16), v)
    m_ref[...] = m_new


def _attn_consume_fixed(s, v, m_ref, acc_ref):
    reps = s.shape[1] // LANES
    p = jnp.exp2(s - jnp.concatenate([m_ref[...]] * reps, axis=1))
    acc_ref[...] += _dot(p.astype(BF16), v)


def _attn_all_keys(consume, qs_ref, k_ref, v_ref, m_ref, acc_ref, s_ref, ck):
    nck = k_ref.shape[0] // ck

    def chunk(c):
        return pl.multiple_of(c * ck, LANES)

    s_ref[...] = _attn_scores(qs_ref, k_ref, 0, ck)

    def body(i, carry):
        s_odd = _attn_scores(qs_ref, k_ref, chunk(2 * i + 1), ck)
        consume(s_ref[...], v_ref[pl.ds(chunk(2 * i), ck), :], m_ref, acc_ref)
        s_ref[...] = _attn_scores(qs_ref, k_ref, chunk(2 * i + 2), ck)
        consume(s_odd, v_ref[pl.ds(chunk(2 * i + 1), ck), :], m_ref, acc_ref)
        return carry

    pairs = (nck - 1) // 2
    lax.fori_loop(0, pairs, body, 0)
    if nck - 1 - 2 * pairs:
        s_last = _attn_scores(qs_ref, k_ref, (nck - 1) * ck, ck)
        consume(s_ref[...], v_ref[(nck - 2) * ck:(nck - 1) * ck, :], m_ref, acc_ref)
        consume(s_last, v_ref[(nck - 1) * ck:nck * ck, :], m_ref, acc_ref)
    else:
        consume(s_ref[...], v_ref[(nck - 1) * ck:nck * ck, :], m_ref, acc_ref)


def _attn_kernel(q_ref, k_ref, v_ref, o_ref, qs_ref, m_ref, acc_ref, s_ref, kmax_ref, *, ctx_len, ck):
    tq = q_ref.shape[0]
    is_ctx_tile = pl.program_id(2) == 0
    for h in range(ATT_GROUP):
        qs_ref[h * tq:(h + 1) * tq, :] = q_ref[:, h * HEAD_DIM:(h + 1) * HEAD_DIM]
    acc_ref[...] = jnp.zeros_like(acc_ref)

    @pl.when(is_ctx_tile)
    def _():
        kf = k_ref[...].astype(F32)
        kmax_ref[...] = jnp.full(kmax_ref.shape, jnp.max(jnp.sum(kf * kf, axis=-1, keepdims=True)))
        m_ref[...] = jnp.full_like(m_ref, -jnp.inf)
        _attn_consume(_attn_scores(qs_ref, k_ref, 0, ctx_len), v_ref[0:ctx_len, :], m_ref, acc_ref)

    @pl.when(jnp.logical_not(is_ctx_tile))
    def _():
        qf = qs_ref[...].astype(F32)
        bound = jnp.sqrt(jnp.sum(qf * qf, axis=-1, keepdims=True) * kmax_ref[0:1, 0:1])
        bound_is_safe = jnp.max(bound) <= ATTN_SAFE_SHIFT

        @pl.when(bound_is_safe)
        def _():
            m_ref[...] = jnp.zeros_like(m_ref) + bound
            _attn_all_keys(_attn_consume_fixed, qs_ref, k_ref, v_ref, m_ref, acc_ref, s_ref, ck)

        @pl.when(jnp.logical_not(bound_is_safe))
        def _():
            m_ref[...] = jnp.full_like(m_ref, -jnp.inf)
            _attn_all_keys(_attn_consume, qs_ref, k_ref, v_ref, m_ref, acc_ref, s_ref, ck)

    acc = acc_ref[...]
    out = acc[:, :HEAD_DIM] / acc[:, HEAD_DIM:]
    for h in range(ATT_GROUP):
        o_ref[:, h * HEAD_DIM:(h + 1) * HEAD_DIM] = out[h * tq:(h + 1) * tq].astype(o_ref.dtype)


def _attn(qn, kn, vn, ctx_len):
    b, s, dq = qn.shape
    tq = SEQ_BLOCK
    ck = _first_divisor(s, (768, 640, 256))
    gw = ATT_GROUP * HEAD_DIM
    rows = ATT_GROUP * tq
    return pl.pallas_call(
        functools.partial(_attn_kernel, ctx_len=ctx_len, ck=ck),
        grid=(b, ATT_KV_HEADS, s // tq),
        in_specs=[
            pl.BlockSpec((None, tq, gw), lambda bi, h, i: (bi, i, h)),
            pl.BlockSpec((None, s, HEAD_DIM), lambda bi, h, i: (bi, 0, h)),
            pl.BlockSpec((None, s, 2 * HEAD_DIM), lambda bi, h, i: (bi, 0, h)),
        ],
        out_specs=pl.BlockSpec((None, tq, gw), lambda bi, h, i: (bi, i, h)),
        out_shape=jax.ShapeDtypeStruct((b, s, dq), BF16),
        scratch_shapes=[
            pltpu.VMEM((rows, HEAD_DIM), BF16),
            pltpu.VMEM((rows, LANES), F32),
            pltpu.VMEM((rows, 2 * HEAD_DIM), F32),
            pltpu.VMEM((rows, ck), F32),
            pltpu.VMEM((SUBLANES, LANES), F32),
        ],
        compiler_params=_params("parallel", "parallel", "arbitrary"),
        name="attn",
    )(qn, kn, vn)


def _merge_kernel(x_ref, y1_ref, y2_ref, y3_ref, bg_ref, modb_ref, modc_ref, wb_ref, wo_ref, o_ref,
                  *, tm, ctx_len):
    d = x_ref.shape[1]
    gate = jax.nn.sigmoid(bg_ref[...].astype(F32))
    m = (gate[:, 0:d] * _dot(y1_ref[...], wb_ref[0])
         + gate[:, d:2 * d] * _dot(y2_ref[...], wb_ref[1])
         + gate[:, 2 * d:3 * d] * _dot(y3_ref[...], wb_ref[2]))
    upd = _dot(m.astype(BF16), wo_ref[...])
    is_ctx = _ctx_rows(pl.program_id(1), tm, ctx_len)
    ga = jnp.where(is_ctx, modc_ref[2:3, :], modb_ref[2:3, :])
    o_ref[...] = x_ref[...] + ga * upd


def _merge(xs, y1, y2, y3, p, mods, wb, wo, ctx_len):
    b, s, d = xs.shape
    tm = _first_divisor(s, (768, 640, 256))
    nb = mods.shape[0] - 1
    row = lambda bi, i: (bi, i, 0)
    return pl.pallas_call(
        functools.partial(_merge_kernel, tm=tm, ctx_len=ctx_len),
        grid=(b, s // tm),
        in_specs=[
            pl.BlockSpec((None, tm, d), row),
            pl.BlockSpec((None, tm, d), row),
            pl.BlockSpec((None, tm, d), row),
            pl.BlockSpec((None, tm, d), row),
            pl.BlockSpec((None, tm, N_BRANCHES * d), lambda bi, i: (bi, i, P_BG // (N_BRANCHES * d))),
            pl.BlockSpec((None, 6, d), lambda bi, i: (bi, 0, 0)),
            pl.BlockSpec((None, 6, d), lambda bi, i: (nb, 0, 0)),
            pl.BlockSpec((N_BRANCHES, d, d), lambda bi, i: (0, 0, 0)),
            pl.BlockSpec((d, d), lambda bi, i: (0, 0)),
        ],
        out_specs=pl.BlockSpec((None, tm, d), row),
        out_shape=jax.ShapeDtypeStruct((b, s, d), F32),
        compiler_params=_params("parallel", "parallel"),
        name="merge",
    )(xs, y1, y2, y3, p, mods, mods, wb, wo)


def _mlp_kernel(x_ref, modb_ref, modc_ref, g_ref, w1_ref, w2_ref, o_ref, h_ref, acc_ref, *, tm, ctx_len):
    i = pl.program_id(1)
    j = pl.program_id(2)
    is_ctx = _ctx_rows(i, tm, ctx_len)

    @pl.when(j == 0)
    def _():
        y = _rms(x_ref[...]) * g_ref[...]
        shift = jnp.where(is_ctx, modc_ref[3:4, :], modb_ref[3:4, :])
        scale = jnp.where(is_ctx, modc_ref[4:5, :], modb_ref[4:5, :])
        h_ref[...] = (y * (1.0 + scale) + shift).astype(BF16)
        acc_ref[...] = jnp.zeros_like(acc_ref)

    t = jnp.maximum(_dot(h_ref[...], w1_ref[...]), 0.0)
    acc_ref[...] += _dot((t * t).astype(BF16), w2_ref[...])

    @pl.when(j == pl.num_programs(2) - 1)
    def _():
        ga = jnp.where(is_ctx, modc_ref[5:6, :], modb_ref[5:6, :])
        o_ref[...] = x_ref[...] + ga * acc_ref[...]


def _mlp(xs, mods, norm_g, w1, w2, ctx_len):
    b, s, d = xs.shape
    dff = w1.shape[1]
    tm = _first_divisor(s, (1408, 1280, 640, 256))
    tf = _first_divisor(dff, (1024, 512))
    nb = mods.shape[0] - 1
    return pl.pallas_call(
        functools.partial(_mlp_kernel, tm=tm, ctx_len=ctx_len),
        grid=(b, s // tm, dff // tf),
        in_specs=[
            pl.BlockSpec((None, tm, d), lambda bi, i, j: (bi, i, 0)),
            pl.BlockSpec((None, 6, d), lambda bi, i, j: (bi, 0, 0)),
            pl.BlockSpec((None, 6, d), lambda bi, i, j: (nb, 0, 0)),
            pl.BlockSpec((1, d), lambda bi, i, j: (0, 0)),
            pl.BlockSpec((d, tf), lambda bi, i, j: (0, j)),
            pl.BlockSpec((tf, d), lambda bi, i, j: (j, 0)),
        ],
        out_specs=pl.BlockSpec((None, tm, d), lambda bi, i, j: (bi, i, 0)),
        out_shape=jax.ShapeDtypeStruct((b, s, d), F32),
        scratch_shapes=[pltpu.VMEM((tm, d), BF16), pltpu.VMEM((tm, d), F32)],
        compiler_params=_params("parallel", "parallel", "arbitrary"),
        name="mlp",
    )(xs, mods, mods, norm_g.reshape(1, d), w1, w2)


def _pack_w_in(w_in):
    lx, ly, gq, gk, gv, gr, gz, aq, ak, av, bg = jnp.split(
        w_in, (1024, 2048, 2560, 3072, 4096, 5120, 5152, 6176, 6432, 6688), axis=-1)
    pad = jnp.zeros(w_in.shape[:-1] + (P_GZ_PAD - gz.shape[-1],), w_in.dtype)
    perm = _head_perm()

    def permute_heads(w):
        heads = w.reshape(w.shape[:-1] + (w.shape[-1] // HEAD_DIM, HEAD_DIM))
        return heads[..., perm].reshape(w.shape)

    return jnp.concatenate([lx, ly, gv, gr, permute_heads(aq), gq, gk, bg, permute_heads(ak), av, gz, pad],
                           axis=-1).astype(BF16)


def _block_diag(w):
    per = MXU_DIM // LRU_BLOCK_W
    lead = w.shape[:-3]
    wg = w.reshape(lead + (w.shape[-3] // per, per, LRU_BLOCK_W, LRU_BLOCK_W))
    eye = jnp.eye(per, dtype=w.dtype)
    bd = jnp.einsum("...gaij,ac->...gaicj", wg, eye)
    return bd.reshape(lead + (w.shape[-3] // per, MXU_DIM, MXU_DIM)).astype(BF16)


def _pad_gate_w(gate_w):
    n_layers, n_dir, rank, dk = gate_w.shape
    out = jnp.zeros((n_layers, n_dir, LANES, dk), gate_w.dtype)
    for d in range(n_dir):
        out = out.at[:, d, d * rank:(d + 1) * rank].set(gate_w[:, d])
    return out.astype(BF16)


def kernel(x, c, ctx, c_ctx, mod_w, mod_b, norm1_g, norm2_g, w_in, conv_w, conv_b, lru_a_w, lru_a_b,
           lru_x_w, lru_x_b, lru_lambda, gla_gate_w, gla_gate_b, gla_norm_g, q_norm_g, k_norm_g,
           w_branch, w_out, mlp_w1, mlp_w2):
    n_lat = x.shape[1]
    ctx_len = ctx.shape[1]
    assert x.shape[2] == D and ctx_len == SEQ_BLOCK and n_lat % SEQ_BLOCK == 0
    n_layers = mod_w.shape[0]

    xs = jnp.concatenate([ctx, x], axis=1)
    mods = _modulation(c, c_ctx, mod_w, mod_b)
    cos, sin = _rope_tables(n_lat, ctx_len)
    perm = _head_perm()
    wp = _pack_w_in(w_in)
    wa = _block_diag(lru_a_w)
    wx = _block_diag(lru_x_w)
    wg = _pad_gate_w(gla_gate_w)
    wb = w_branch.astype(BF16)
    wo = w_out.astype(BF16)
    w1 = mlp_w1.astype(BF16)
    w2 = mlp_w2.astype(BF16)

    for l in range(n_layers):
        p = _proj(xs, mods[l], norm1_g[l], wp[l], ctx_len)
        lru = lambda dr, **kw: _lru(p, conv_w[l], conv_b[l], wa[l, dr], lru_a_b[l, dr], wx[l, dr],
                                    lru_x_b[l, dr], lru_lambda[l, dr], rev=bool(dr), **kw)
        y1 = lru(1, hf=lru(0))
        gla = lambda dr, **kw: _gla(p, wg[l, dr], gla_gate_b[l, dr], rev=bool(dr), **kw)
        y2 = gla(1, of=gla(0), norm_g=gla_norm_g[l])
        qn, kn, vn = _qkv(p, cos, sin, q_norm_g[l][perm], k_norm_g[l][perm])
        y3 = _attn(qn, kn, vn, ctx_len)
        x1 = _merge(xs, y1, y2, y3, p, mods[l], wb[l], wo[l], ctx_len)
        xs = _mlp(x1, mods[l], norm2_g[l], w1[l], w2[l], ctx_len)
    return xs[:, ctx_len:]
```

```python
import functools

import jax
import jax.numpy as jnp
from jax import lax
from jax.experimental import pallas as pl
from jax.experimental.pallas import tpu as pltpu

F32 = jnp.float32
BF16 = jnp.bfloat16

NORM_EPS = 1e-6
F32_TINY = 1e-37
GRID_W = 64
ROPE_THETA = 10000.0
RGLRU_C = 8.0
LRU_BLOCK_W = 64
CONV_WIDTH = 4
GLA_HEADS = 4
GLA_DK = 128
GLA_DV = 256
GLA_GATE_RANK = 16
GLA_TAU = 16.0
GLA_CHUNK = 64
ATT_HEADS = 8
ATT_KV_HEADS = 2
ATT_GROUP = ATT_HEADS // ATT_KV_HEADS
HEAD_DIM = 128
N_BRANCHES = 3
Q_SCALE_LOG2 = HEAD_DIM ** -0.5 * 1.4426950408889634
ATTN_SAFE_SHIFT = 40.0

LANES = 128
SUBLANES = 8
MXU_DIM = 256
VMEM_LIMIT_BYTES = 56 * 1024 * 1024

SEQ_BLOCK = 256
HALO = 2 * SUBLANES

D = 1024
P_LX, P_LY, P_GV, P_GR, P_AQ = 0, 1024, 2048, 3072, 4096
P_GQ, P_GK, P_BG = 5120, 5632, 6144
P_AK, P_AV, P_GZ = 9216, 9472, 9728
P_GZ_PAD = 256
P_WIDTH = P_GZ + P_GZ_PAD
PROJ_TN = 3328


def _first_divisor(n, candidates):
    for c in candidates:
        if n % c == 0:
            return c
    raise ValueError(f"no tile in {candidates} divides {n}")


def _params(*sem):
    return pltpu.CompilerParams(dimension_semantics=sem, vmem_limit_bytes=VMEM_LIMIT_BYTES)


def _rms(x):
    return x * lax.rsqrt(jnp.mean(x * x, axis=-1, keepdims=True) + NORM_EPS)


def _dot(a, b):
    return jnp.dot(a, b, preferred_element_type=F32)


def _dot_nt(a, b):
    return lax.dot_general(a, b, (((1,), (1,)), ((), ())), preferred_element_type=F32)


def _dot_tn(a, b):
    return lax.dot_general(a, b, (((0,), (0,)), ((), ())), preferred_element_type=F32)


def _ctx_rows(tile_idx, tm, ctx_len):
    rows = tile_idx * tm + lax.broadcasted_iota(jnp.int32, (tm, 1), 0)
    return rows < ctx_len


def _mod_kernel(c_ref, w_ref, b_ref, o_ref):
    c = c_ref[...]
    cs = (c * jax.nn.sigmoid(c)).astype(BF16)
    o_ref[...] = _dot(cs, w_ref[...].astype(BF16)) + b_ref[...]


def _modulation(c, c_ctx, mod_w, mod_b):
    n_layers, d, n6 = mod_w.shape
    b = c.shape[0]
    rows = -(-(b + 1) // SUBLANES) * SUBLANES
    cs = jnp.zeros((rows, d), F32).at[:b].set(c).at[b].set(c_ctx)
    tn = _first_divisor(n6, (1536, 1024, 512))
    out = pl.pallas_call(
        _mod_kernel,
        grid=(n_layers, n6 // tn),
        in_specs=[
            pl.BlockSpec((rows, d), lambda l, j: (0, 0)),
            pl.BlockSpec((None, d, tn), lambda l, j: (l, 0, j)),
            pl.BlockSpec((None, 1, tn), lambda l, j: (l, 0, j)),
        ],
        out_specs=pl.BlockSpec((None, rows, tn), lambda l, j: (l, 0, j)),
        out_shape=jax.ShapeDtypeStruct((n_layers, rows, n6), F32),
        compiler_params=_params("parallel", "parallel"),
        name="modulation",
    )(cs, mod_w, mod_b.reshape(n_layers, 1, n6))
    return out.reshape(n_layers, rows, 6, d)[:, :b + 1]


def _proj_kernel(x_ref, modb_ref, modc_ref, g_ref, w_ref, o_ref, h_ref, *, tm, ctx_len):
    i = pl.program_id(1)

    @pl.when(pl.program_id(2) == 0)
    def _():
        y = _rms(x_ref[...]) * g_ref[...]
        is_ctx = _ctx_rows(i, tm, ctx_len)
        shift = jnp.where(is_ctx, modc_ref[0:1, :], modb_ref[0:1, :])
        scale = jnp.where(is_ctx, modc_ref[1:2, :], modb_ref[1:2, :])
        h_ref[...] = (y * (1.0 + scale) + shift).astype(BF16)

    o_ref[...] = _dot(h_ref[...], w_ref[...]).astype(o_ref.dtype)


def _proj(xs, mods, norm_g, wp, ctx_len):
    b, s, d = xs.shape
    tm = _first_divisor(s, (1408, 1280, 640, 256))
    nb = mods.shape[0] - 1
    return pl.pallas_call(
        functools.partial(_proj_kernel, tm=tm, ctx_len=ctx_len),
        grid=(b, s // tm, P_WIDTH // PROJ_TN),
        in_specs=[
            pl.BlockSpec((None, tm, d), lambda bi, i, j: (bi, i, 0)),
            pl.BlockSpec((None, 6, d), lambda bi, i, j: (bi, 0, 0)),
            pl.BlockSpec((None, 6, d), lambda bi, i, j: (nb, 0, 0)),
            pl.BlockSpec((1, d), lambda bi, i, j: (0, 0)),
            pl.BlockSpec((d, PROJ_TN), lambda bi, i, j: (0, j)),
        ],
        out_specs=pl.BlockSpec((None, tm, PROJ_TN), lambda bi, i, j: (bi, i, j)),
        out_shape=jax.ShapeDtypeStruct((b, s, P_WIDTH), BF16),
        scratch_shapes=[pltpu.VMEM((tm, d), BF16)],
        compiler_params=_params("parallel", "parallel", "arbitrary"),
        name="proj",
    )(xs, mods, mods, norm_g.reshape(1, d), wp)


def _seq_block(tb, nblk, rev):
    if not rev:
        return tb
    return jnp.where(tb == 0, 0, nblk - tb)


def _per_sample(block_fn, shared):
    def body(*refs, **kw):
        for bb in range(refs[0].shape[0]):
            block_fn(*[r if i in shared else r.at[bb] for i, r in enumerate(refs)], **kw)
    return body


BATCH_TILE = (2, 1)


def _gelu_tanh(x):
    return 0.5 * x * (1.0 + jnp.tanh(0.7978845608028654 * (x + 0.044715 * (x * x * x))))


def _segment_perm(r, inverse):
    seg = r // SUBLANES
    ri = lax.broadcasted_iota(jnp.int32, (r, r), 0)
    ci = lax.broadcasted_iota(jnp.int32, (r, r), 1)
    if inverse:
        ri, ci = ci, ri
    src = (ri & (SUBLANES - 1)) * seg + lax.shift_right_logical(ri, 3)
    return jnp.where(ci == src, 1.0, 0.0).astype(BF16)


def _sublane_scan(a, u, carry, sub, rev):
    for k in (1, 2, 4):
        if rev:
            shift, valid = SUBLANES - k, sub < SUBLANES - k
        else:
            shift, valid = k, sub >= k
        a_sh = pltpu.roll(a, shift, 0)
        u_sh = pltpu.roll(u, shift, 0)
        u = u + a * jnp.where(valid, u_sh, 0.0)
        a = a * jnp.where(valid, a_sh, 1.0)
    return a * carry + u


def _lru_kernel(*refs, rev, nblk):
    if rev:
        (lx_ref, prev_ref, next_ref, cw_ref, cb_ref, wa_ref, ab_ref, wx_ref, xb_ref, lam_ref,
         hf_ref, ly_ref, o_ref, carry_ref) = refs
    else:
        (lx_ref, prev_ref, next_ref, cw_ref, cb_ref, wa_ref, ab_ref, wx_ref, xb_ref, lam_ref,
         o_ref, carry_ref) = refs
    tb = pl.program_id(1)
    te = _seq_block(tb, nblk, rev)
    r, w = lx_ref.shape
    seg = r // SUBLANES
    top = SUBLANES - 1

    @pl.when(tb == 0)
    def _():
        carry_ref[...] = jnp.zeros_like(carry_ref)

    perm = _segment_perm(r, inverse=False)
    x = _dot(perm, lx_ref[...])
    sub = lax.broadcasted_iota(jnp.int32, (SUBLANES, w), 0)

    has_prev = te >= 2
    has_next = jnp.logical_and(te >= 1, te <= nblk - 2)
    prev_rows = jnp.where(has_prev, prev_ref[...].astype(F32), 0.0)
    next_rows = jnp.where(has_next, next_ref[...].astype(F32), 0.0)
    edge_m1 = jnp.where(sub == 0, prev_rows[HALO - 1:HALO], pltpu.roll(x[r - SUBLANES:r], 1, 0))
    edge_p1 = jnp.where(sub == top, next_rows[0:1], pltpu.roll(x[0:SUBLANES], top, 0))
    edge_p2 = jnp.where(sub == top, next_rows[1:2], pltpu.roll(x[SUBLANES:2 * SUBLANES], top, 0))
    x_m1 = jnp.concatenate([edge_m1, x[:r - SUBLANES]], axis=0)
    x_p1 = jnp.concatenate([x[SUBLANES:], edge_p1], axis=0)
    x_p2 = jnp.concatenate([x[2 * SUBLANES:], edge_p1, edge_p2], axis=0)
    xc = (cw_ref[0:1, :] * x_m1 + cw_ref[1:2, :] * x + cw_ref[2:3, :] * x_p1 + cw_ref[3:4, :] * x_p2
          + cb_ref[...])

    xcb = xc.astype(BF16)
    lam = lam_ref[...]
    log_a_scale = -RGLRU_C * (jnp.maximum(-lam, 0.0) + jnp.log1p(jnp.exp(-jnp.abs(lam))))
    a_parts, u_parts = [], []
    for gi in range(w // MXU_DIM):
        sl = slice(gi * MXU_DIM, (gi + 1) * MXU_DIM)
        rg = jax.nn.sigmoid(_dot(xcb[:, sl], wa_ref[gi]) + ab_ref[:, sl])
        ig = jax.nn.sigmoid(_dot(xcb[:, sl], wx_ref[gi]) + xb_ref[:, sl])
        log_a = rg * log_a_scale[:, sl]
        a_g = jnp.exp(log_a)
        a_parts.append(a_g)
        one_minus_a2 = -jnp.tanh(log_a) * (1.0 + a_g * a_g)
        root = one_minus_a2 * lax.rsqrt(jnp.maximum(one_minus_a2, F32_TINY))
        u_parts.append(root * (ig * xc[:, sl]))
    a = jnp.concatenate(a_parts, axis=1)
    u = jnp.concatenate(u_parts, axis=1)

    steps = range(seg)
    h_loc = [None] * seg
    p_cum = [None] * seg
    hl = pc = None
    for i in (reversed(steps) if rev else steps):
        rows = slice(i * SUBLANES, (i + 1) * SUBLANES)
        hl = u[rows] if hl is None else a[rows] * hl + u[rows]
        pc = a[rows] if pc is None else a[rows] * pc
        h_loc[i], p_cum[i] = hl, pc
    carry = carry_ref[...]
    h_edge = _sublane_scan(pc, hl, carry, sub, rev)
    if rev:
        seg_in = jnp.where(sub == top, carry, pltpu.roll(h_edge, top, 0))
        carry_ref[...] = h_edge[0:1, :]
    else:
        seg_in = jnp.where(sub == 0, carry, pltpu.roll(h_edge, 1, 0))
        carry_ref[...] = h_edge[top:SUBLANES, :]
    h_all = jnp.concatenate([h_loc[i] + p_cum[i] * seg_in for i in steps], axis=0)
    if rev:
        gate = _gelu_tanh(_dot(perm, ly_ref[...]))
        y = ((hf_ref[...] + h_all) * gate).astype(BF16)
        o_ref[...] = _dot(_segment_perm(r, inverse=True), y).astype(o_ref.dtype)
    else:
        o_ref[...] = h_all


def _lru(p, conv_w, conv_b, wa, ab, wx, xb, lam, rev, hf=None):
    b, s, _ = p.shape
    w = conv_w.shape[1]
    r = SEQ_BLOCK
    nblk = s // r
    per_halo = r // HALO
    last_halo = s // HALO - 1
    blk = lambda tb: _seq_block(tb, nblk, rev)
    ng = w // MXU_DIM
    bt = _first_divisor(b, BATCH_TILE)
    in_specs = [
        pl.BlockSpec((bt, r, w), lambda bi, tb: (bi, blk(tb), P_LX // w)),
        pl.BlockSpec((bt, HALO, w), lambda bi, tb: (bi, jnp.maximum(blk(tb) * per_halo - 1, 0), P_LX // w)),
        pl.BlockSpec((bt, HALO, w),
                     lambda bi, tb: (bi, jnp.minimum((blk(tb) + 1) * per_halo, last_halo), P_LX // w)),
        pl.BlockSpec((CONV_WIDTH, w), lambda bi, tb: (0, 0)),
        pl.BlockSpec((1, w), lambda bi, tb: (0, 0)),
        pl.BlockSpec((ng, MXU_DIM, MXU_DIM), lambda bi, tb: (0, 0, 0)),
        pl.BlockSpec((1, w), lambda bi, tb: (0, 0)),
        pl.BlockSpec((ng, MXU_DIM, MXU_DIM), lambda bi, tb: (0, 0, 0)),
        pl.BlockSpec((1, w), lambda bi, tb: (0, 0)),
        pl.BlockSpec((1, w), lambda bi, tb: (0, 0)),
    ]
    args = [p, p, p, conv_w, conv_b.reshape(1, w), wa, ab.reshape(1, w), wx, xb.reshape(1, w),
            lam.reshape(1, w)]
    if rev:
        in_specs += [
            pl.BlockSpec((bt, r, w), lambda bi, tb: (bi, blk(tb), 0)),
            pl.BlockSpec((bt, r, w), lambda bi, tb: (bi, blk(tb), P_LY // w)),
        ]
        args += [hf, p]
    return pl.pallas_call(
        functools.partial(_per_sample(_lru_kernel, shared=range(3, 10)), rev=rev, nblk=nblk),
        grid=(b // bt, nblk),
        in_specs=in_specs,
        out_specs=pl.BlockSpec((bt, r, w), lambda bi, tb: (bi, blk(tb), 0)),
        out_shape=jax.ShapeDtypeStruct((b, s, w), BF16 if rev else F32),
        scratch_shapes=[pltpu.VMEM((bt, 1, w), F32)],
        compiler_params=_params("arbitrary", "arbitrary"),
        name="lru_rev" if rev else "lru_fwd",
    )(*args)


def _split3(x):
    hi = x.astype(BF16)
    r1 = x - hi.astype(F32)
    mid = r1.astype(BF16)
    lo = (r1 - mid.astype(F32)).astype(BF16)
    return hi, mid, lo


def _gla_kernel(*refs, rev):
    if rev:
        q_ref, k_ref, v_ref, z_ref, wg_ref, gb_ref, of_ref, r_ref, ng_ref, o_ref, st_ref = refs
    else:
        q_ref, k_ref, v_ref, z_ref, wg_ref, gb_ref, o_ref, st_ref = refs
    r = q_ref.shape[0]
    c = GLA_CHUNK
    nch = r // c

    @pl.when(pl.program_id(1) == 0)
    def _():
        st_ref[...] = jnp.zeros_like(st_ref)

    zz = _dot(z_ref[...], wg_ref[...]) + gb_ref[...]
    g = (jnp.minimum(zz, 0.0) - jnp.log1p(jnp.exp(-jnp.abs(zz)))) * (1.0 / GLA_TAU)

    ri = lax.broadcasted_iota(jnp.int32, (r, r), 0)
    ci = lax.broadcasted_iota(jnp.int32, (r, r), 1)
    same = lax.shift_right_logical(ri, 6) == lax.shift_right_logical(ci, 6)
    tri = jnp.logical_and(same, (ci >= ri) if rev else (ci <= ri))
    tri_b = jnp.where(tri, 1.0, 0.0).astype(BF16)
    bcum = sum(_dot(tri_b, part) for part in _split3(g))

    chunks = range(nch)
    for h in range(GLA_HEADS):
        hl = slice(h * GLA_DK, (h + 1) * GLA_DK)
        vl = slice(h * GLA_DV, (h + 1) * GLA_DV)
        q = q_ref[:, hl].astype(F32) * (GLA_DK ** -0.5)
        k = k_ref[:, hl].astype(F32)
        b_h = bcum[:, hl]
        qd, kd, kl, qe, dec = [], [], [], [], []
        for ch in chunks:
            rows = slice(ch * c, (ch + 1) * c)
            b_c = b_h[rows]
            if rev:
                last, mid = b_c[0:1], b_c[c // 2:c // 2 + 1]
            else:
                last, mid = b_c[c - 1:c], b_c[c // 2 - 1:c // 2]
            qd.append(q[rows] * jnp.exp(b_c - mid))
            kd.append(k[rows] * jnp.exp(mid - b_c))
            kl.append((k[rows] * jnp.exp(last - b_c)).astype(BF16))
            qe.append((q[rows] * jnp.exp(b_c)).astype(BF16))
            dec.append(jnp.exp(last))
        qd = jnp.concatenate(qd, axis=0).astype(BF16)
        kd = jnp.concatenate(kd, axis=0).astype(BF16)
        vb = v_ref[:, vl]
        sc = jnp.where(tri, _dot_nt(qd, kd), 0.0)
        o_h = _dot(sc.astype(BF16), vb)
        st = st_ref[h]
        o_parts = [None] * nch
        for ch in (reversed(chunks) if rev else chunks):
            rows = slice(ch * c, (ch + 1) * c)
            o_parts[ch] = o_h[rows] + _dot_nt(qe[ch], st.astype(BF16))
            st = st * dec[ch] + _dot_tn(vb[rows], kl[ch])
        st_ref[h] = st
        o = jnp.concatenate(o_parts, axis=0)
        if rev:
            o = _rms(of_ref[:, vl] + o) * ng_ref[...]
            rg = r_ref[:, vl].astype(F32)
            o = o * (rg * jax.nn.sigmoid(rg))
        o_ref[:, vl] = o.astype(o_ref.dtype)


def _gla(p, wg, gb, rev, of=None, norm_g=None):
    b, s, _ = p.shape
    r = SEQ_BLOCK
    nblk = s // r
    dk = GLA_HEADS * GLA_DK
    dv = GLA_HEADS * GLA_DV
    blk = lambda tb: _seq_block(tb, nblk, rev)
    bt = _first_divisor(b, BATCH_TILE)
    in_specs = [
        pl.BlockSpec((bt, r, dk), lambda bi, tb: (bi, blk(tb), P_GQ // dk)),
        pl.BlockSpec((bt, r, dk), lambda bi, tb: (bi, blk(tb), P_GK // dk)),
        pl.BlockSpec((bt, r, dv), lambda bi, tb: (bi, blk(tb), P_GV // dv)),
        pl.BlockSpec((bt, r, LANES), lambda bi, tb: (bi, blk(tb), P_GZ // LANES)),
        pl.BlockSpec((LANES, dk), lambda bi, tb: (0, 0)),
        pl.BlockSpec((1, dk), lambda bi, tb: (0, 0)),
    ]
    args = [p, p, p, p, wg, gb.reshape(1, dk)]
    if rev:
        in_specs += [
            pl.BlockSpec((bt, r, dv), lambda bi, tb: (bi, blk(tb), 0)),
            pl.BlockSpec((bt, r, dv), lambda bi, tb: (bi, blk(tb), P_GR // dv)),
            pl.BlockSpec((1, GLA_DV), lambda bi, tb: (0, 0)),
        ]
        args += [of, p, norm_g.reshape(1, GLA_DV)]
    return pl.pallas_call(
        functools.partial(_per_sample(_gla_kernel, shared=(4, 5, 8)), rev=rev),
        grid=(b // bt, nblk),
        in_specs=in_specs,
        out_specs=pl.BlockSpec((bt, r, dv), lambda bi, tb: (bi, blk(tb), 0)),
        out_shape=jax.ShapeDtypeStruct((b, s, dv), BF16 if rev else F32),
        scratch_shapes=[pltpu.VMEM((bt, GLA_HEADS, GLA_DV, GLA_DK), F32)],
        compiler_params=_params("arbitrary", "arbitrary"),
        name="gla_rev" if rev else "gla_fwd",
    )(*args)


def _head_perm():
    quarter = HEAD_DIM // 4
    idx = jnp.arange(HEAD_DIM).reshape(4, quarter)
    return jnp.concatenate([idx[0], idx[2], idx[1], idx[3]])


def _rope_tables(n_lat, ctx_len):
    half = HEAD_DIM // 4
    t = jnp.arange(n_lat)
    inv_freq = ROPE_THETA ** (-jnp.arange(0, 2 * half, 2, dtype=F32) / (2 * half))
    ang_r = (t // GRID_W).astype(F32)[:, None] * inv_freq[None, :]
    ang_c = (t % GRID_W).astype(F32)[:, None] * inv_freq[None, :]
    cos = jnp.concatenate([jnp.cos(ang_r), jnp.cos(ang_c)] * 2, axis=-1)
    sin = jnp.concatenate([-jnp.sin(ang_r), -jnp.sin(ang_c), jnp.sin(ang_r), jnp.sin(ang_c)], axis=-1)
    cos = jnp.concatenate([jnp.ones((ctx_len, HEAD_DIM), F32), cos], axis=0)
    sin = jnp.concatenate([jnp.zeros((ctx_len, HEAD_DIM), F32), sin], axis=0)
    return cos, sin


def _qkv_kernel(aq_ref, ak_ref, av_ref, cos_ref, sin_ref, qg_ref, kg_ref, q_ref, k_ref, v_ref):
    tm = aq_ref.shape[0]
    cos, sin = cos_ref[...], sin_ref[...]

    def norm_rope(x, g):
        y = _rms(x.astype(F32)) * g
        return y * cos + pltpu.roll(y, HEAD_DIM // 2, 1) * sin

    for h in range(ATT_HEADS):
        sl = slice(h * HEAD_DIM, (h + 1) * HEAD_DIM)
        q_ref[:, sl] = (norm_rope(aq_ref[:, sl], qg_ref[...]) * Q_SCALE_LOG2).astype(BF16)
    for h in range(ATT_KV_HEADS):
        sl = slice(h * HEAD_DIM, (h + 1) * HEAD_DIM)
        k_ref[:, sl] = norm_rope(ak_ref[:, sl], kg_ref[...]).astype(BF16)
        v_ref[:, 2 * h * HEAD_DIM:(2 * h + 1) * HEAD_DIM] = av_ref[:, sl]
        v_ref[:, (2 * h + 1) * HEAD_DIM:(2 * h + 2) * HEAD_DIM] = jnp.ones((tm, HEAD_DIM), BF16)


def _qkv(p, cos, sin, q_norm_g, k_norm_g):
    b, s, _ = p.shape
    tm = _first_divisor(s, (768, 256))
    dq = ATT_HEADS * HEAD_DIM
    dkv = ATT_KV_HEADS * HEAD_DIM
    return pl.pallas_call(
        _qkv_kernel,
        grid=(b, s // tm),
        in_specs=[
            pl.BlockSpec((None, tm, dq), lambda bi, i: (bi, i, P_AQ // dq)),
            pl.BlockSpec((None, tm, dkv), lambda bi, i: (bi, i, P_AK // dkv)),
            pl.BlockSpec((None, tm, dkv), lambda bi, i: (bi, i, P_AV // dkv)),
            pl.BlockSpec((tm, HEAD_DIM), lambda bi, i: (i, 0)),
            pl.BlockSpec((tm, HEAD_DIM), lambda bi, i: (i, 0)),
            pl.BlockSpec((1, HEAD_DIM), lambda bi, i: (0, 0)),
            pl.BlockSpec((1, HEAD_DIM), lambda bi, i: (0, 0)),
        ],
        out_specs=[
            pl.BlockSpec((None, tm, dq), lambda bi, i: (bi, i, 0)),
            pl.BlockSpec((None, tm, dkv), lambda bi, i: (bi, i, 0)),
            pl.BlockSpec((None, tm, 2 * dkv), lambda bi, i: (bi, i, 0)),
        ],
        out_shape=[
            jax.ShapeDtypeStruct((b, s, dq), BF16),
            jax.ShapeDtypeStruct((b, s, dkv), BF16),
            jax.ShapeDtypeStruct((b, s, 2 * dkv), BF16),
        ],
        compiler_params=_params("parallel", "parallel"),
        name="qkv",
    )(p, p, p, cos, sin, q_norm_g.reshape(1, HEAD_DIM), k_norm_g.reshape(1, HEAD_DIM))


def _attn_scores(qs_ref, k_ref, start, size):
    return _dot_nt(qs_ref[...], k_ref[pl.ds(start, size), :])


def _attn_consume(s, v, m_ref, acc_ref):
    reps = s.shape[1] // LANES
    m_prev = m_ref[...]
    m_new = jnp.maximum(m_prev, jnp.max(s, axis=-1, keepdims=True))
    p = jnp.exp2(s - jnp.concatenate([m_new] * reps, axis=1))
    alpha = jnp.exp2(m_prev - m_new)
    acc_ref[...] = jnp.concatenate([alpha] * 2, axis=1) * acc_ref[...] + _dot(p.astype(BF16), v)
    m_ref[...] = m_new


def _attn_consume_fixed(s, v, shift, acc_ref):
    acc_ref[...] += _dot(jnp.exp2(s - shift).astype(BF16), v)


def _attn_all_keys(consume, qs_ref, k_ref, v_ref, m_ref, acc_ref, s_ref, ck):
    nck = k_ref.shape[0] // ck

    def chunk(c):
        return pl.multiple_of(c * ck, LANES)

    s_ref[...] = _attn_scores(qs_ref, k_ref, 0, ck)

    def body(i, carry):
        s_odd = _attn_scores(qs_ref, k_ref, chunk(2 * i + 1), ck)
        consume(s_ref[...], v_ref[pl.ds(chunk(2 * i), ck), :], m_ref, acc_ref)
        s_ref[...] = _attn_scores(qs_ref, k_ref, chunk(2 * i + 2), ck)
        consume(s_odd, v_ref[pl.ds(chunk(2 * i + 1), ck), :], m_ref, acc_ref)
        return carry

    pairs = (nck - 1) // 2
    lax.fori_loop(0, pairs, body, 0)
    if nck - 1 - 2 * pairs:
        s_last = _attn_scores(qs_ref, k_ref, (nck - 1) * ck, ck)
        consume(s_ref[...], v_ref[(nck - 2) * ck:(nck - 1) * ck, :], m_ref, acc_ref)
        consume(s_last, v_ref[(nck - 1) * ck:nck * ck, :], m_ref, acc_ref)
    else:
        consume(s_ref[...], v_ref[(nck - 1) * ck:nck * ck, :], m_ref, acc_ref)


def _attn_kernel(qb_ref, q_ref, k_ref, v_ref, o_ref, qs_ref, m_ref, acc_ref, s_ref, shift_ref, *, ctx_len, ck):
    tq = q_ref.shape[0]
    is_ctx_tile = pl.program_id(2) == 0
    for h in range(ATT_GROUP):
        qs_ref[h * tq:(h + 1) * tq, :] = q_ref[:, h * HEAD_DIM:(h + 1) * HEAD_DIM]
    acc_ref[...] = jnp.zeros_like(acc_ref)

    @pl.when(is_ctx_tile)
    def _():
        kf = k_ref[...].astype(F32)
        shift_ref[0] = qb_ref[0, 0] * jnp.max(jnp.sqrt(jnp.sum(kf * kf, axis=-1, keepdims=True)))
        m_ref[...] = jnp.full_like(m_ref, -jnp.inf)
        _attn_consume(_attn_scores(qs_ref, k_ref, 0, ctx_len), v_ref[0:ctx_len, :], m_ref, acc_ref)

    @pl.when(jnp.logical_not(is_ctx_tile))
    def _():
        shift = shift_ref[0]
        bound_is_safe = shift <= ATTN_SAFE_SHIFT

        @pl.when(bound_is_safe)
        def _():
            _attn_all_keys(_attn_consume_fixed, qs_ref, k_ref, v_ref, shift, acc_ref, s_ref, ck)

        @pl.when(jnp.logical_not(bound_is_safe))
        def _():
            m_ref[...] = jnp.full_like(m_ref, -jnp.inf)
            _attn_all_keys(_attn_consume, qs_ref, k_ref, v_ref, m_ref, acc_ref, s_ref, ck)

    acc = acc_ref[...]
    out = acc[:, :HEAD_DIM] / acc[:, HEAD_DIM:]
    for h in range(ATT_GROUP):
        o_ref[:, h * HEAD_DIM:(h + 1) * HEAD_DIM] = out[h * tq:(h + 1) * tq].astype(o_ref.dtype)


def _attn(qn, kn, vn, q_bound, ctx_len):
    b, s, dq = qn.shape
    tq = SEQ_BLOCK
    ck = _first_divisor(s, (768, 640, 256))
    gw = ATT_GROUP * HEAD_DIM
    rows = ATT_GROUP * tq
    return pl.pallas_call(
        functools.partial(_attn_kernel, ctx_len=ctx_len, ck=ck),
        grid=(b, ATT_KV_HEADS, s // tq),
        in_specs=[
            pl.BlockSpec(memory_space=pltpu.SMEM),
            pl.BlockSpec((None, tq, gw), lambda bi, h, i: (bi, i, h)),
            pl.BlockSpec((None, s, HEAD_DIM), lambda bi, h, i: (bi, 0, h)),
            pl.BlockSpec((None, s, 2 * HEAD_DIM), lambda bi, h, i: (bi, 0, h)),
        ],
        out_specs=pl.BlockSpec((None, tq, gw), lambda bi, h, i: (bi, i, h)),
        out_shape=jax.ShapeDtypeStruct((b, s, dq), BF16),
        scratch_shapes=[
            pltpu.VMEM((rows, HEAD_DIM), BF16),
            pltpu.VMEM((rows, LANES), F32),
            pltpu.VMEM((rows, 2 * HEAD_DIM), F32),
            pltpu.VMEM((rows, ck), F32),
            pltpu.SMEM((1,), F32),
        ],
        compiler_params=_params("parallel", "parallel", "arbitrary"),
        name="attn",
    )(q_bound, qn, kn, vn)


def _merge_kernel(x_ref, y1_ref, y2_ref, y3_ref, bg_ref, modb_ref, modc_ref, wb_ref, wo_ref, o_ref,
                  *, tm, ctx_len):
    d = x_ref.shape[1]
    gate = jax.nn.sigmoid(bg_ref[...].astype(F32))
    m = (gate[:, 0:d] * _dot(y1_ref[...], wb_ref[0])
         + gate[:, d:2 * d] * _dot(y2_ref[...], wb_ref[1])
         + gate[:, 2 * d:3 * d] * _dot(y3_ref[...], wb_ref[2]))
    upd = _dot(m.astype(BF16), wo_ref[...])
    is_ctx = _ctx_rows(pl.program_id(1), tm, ctx_len)
    ga = jnp.where(is_ctx, modc_ref[2:3, :], modb_ref[2:3, :])
    o_ref[...] = x_ref[...] + ga * upd


def _merge(xs, y1, y2, y3, p, mods, wb, wo, ctx_len):
    b, s, d = xs.shape
    tm = _first_divisor(s, (768, 640, 256))
    nb = mods.shape[0] - 1
    row = lambda bi, i: (bi, i, 0)
    return pl.pallas_call(
        functools.partial(_merge_kernel, tm=tm, ctx_len=ctx_len),
        grid=(b, s // tm),
        in_specs=[
            pl.BlockSpec((None, tm, d), row),
            pl.BlockSpec((None, tm, d), row),
            pl.BlockSpec((None, tm, d), row),
            pl.BlockSpec((None, tm, d), row),
            pl.BlockSpec((None, tm, N_BRANCHES * d), lambda bi, i: (bi, i, P_BG // (N_BRANCHES * d))),
            pl.BlockSpec((None, 6, d), lambda bi, i: (bi, 0, 0)),
            pl.BlockSpec((None, 6, d), lambda bi, i: (nb, 0, 0)),
            pl.BlockSpec((N_BRANCHES, d, d), lambda bi, i: (0, 0, 0)),
            pl.BlockSpec((d, d), lambda bi, i: (0, 0)),
        ],
        out_specs=pl.BlockSpec((None, tm, d), row),
        out_shape=jax.ShapeDtypeStruct((b, s, d), F32),
        compiler_params=_params("parallel", "parallel"),
        name="merge",
    )(xs, y1, y2, y3, p, mods, mods, wb, wo)


def _mlp_kernel(x_ref, modb_ref, modc_ref, g_ref, w1_ref, w2_ref, o_ref, h_ref, acc_ref, *, tm, ctx_len):
    i = pl.program_id(1)
    j = pl.program_id(2)
    is_ctx = _ctx_rows(i, tm, ctx_len)

    @pl.when(j == 0)
    def _():
        y = _rms(x_ref[...]) * g_ref[...]
        shift = jnp.where(is_ctx, modc_ref[3:4, :], modb_ref[3:4, :])
        scale = jnp.where(is_ctx, modc_ref[4:5, :], modb_ref[4:5, :])
        h_ref[...] = (y * (1.0 + scale) + shift).astype(BF16)
        acc_ref[...] = jnp.zeros_like(acc_ref)

    t = jnp.maximum(_dot(h_ref[...], w1_ref[...]), 0.0)
    acc_ref[...] += _dot((t * t).astype(BF16), w2_ref[...])

    @pl.when(j == pl.num_programs(2) - 1)
    def _():
        ga = jnp.where(is_ctx, modc_ref[5:6, :], modb_ref[5:6, :])
        o_ref[...] = x_ref[...] + ga * acc_ref[...]


def _mlp(xs, mods, norm_g, w1, w2, ctx_len):
    b, s, d = xs.shape
    dff = w1.shape[1]
    tm = _first_divisor(s, (1408, 1280, 640, 256))
    tf = _first_divisor(dff, (1024, 512))
    nb = mods.shape[0] - 1
    return pl.pallas_call(
        functools.partial(_mlp_kernel, tm=tm, ctx_len=ctx_len),
        grid=(b, s // tm, dff // tf),
        in_specs=[
            pl.BlockSpec((None, tm, d), lambda bi, i, j: (bi, i, 0)),
            pl.BlockSpec((None, 6, d), lambda bi, i, j: (bi, 0, 0)),
            pl.BlockSpec((None, 6, d), lambda bi, i, j: (nb, 0, 0)),
            pl.BlockSpec((1, d), lambda bi, i, j: (0, 0)),
            pl.BlockSpec((d, tf), lambda bi, i, j: (0, j)),
            pl.BlockSpec((tf, d), lambda bi, i, j: (j, 0)),
        ],
        out_specs=pl.BlockSpec((None, tm, d), lambda bi, i, j: (bi, i, 0)),
        out_shape=jax.ShapeDtypeStruct((b, s, d), F32),
        scratch_shapes=[pltpu.VMEM((tm, d), BF16), pltpu.VMEM((tm, d), F32)],
        compiler_params=_params("parallel", "parallel", "arbitrary"),
        name="mlp",
    )(xs, mods, mods, norm_g.reshape(1, d), w1, w2)


def _pack_w_in(w_in):
    lx, ly, gq, gk, gv, gr, gz, aq, ak, av, bg = jnp.split(
        w_in.astype(BF16), (1024, 2048, 2560, 3072, 4096, 5120, 5152, 6176, 6432, 6688), axis=-1)
    pad = jnp.zeros(w_in.shape[:-1] + (P_GZ_PAD - gz.shape[-1],), BF16)

    def permute_heads(w):
        q4 = w.reshape(w.shape[:-1] + (w.shape[-1] // HEAD_DIM, 4, HEAD_DIM // 4))
        return jnp.stack([q4[..., 0, :], q4[..., 2, :], q4[..., 1, :], q4[..., 3, :]], axis=-2).reshape(w.shape)

    return jnp.concatenate([lx, ly, gv, gr, permute_heads(aq), gq, gk, bg, permute_heads(ak), av, gz, pad],
                           axis=-1)


def _block_diag(w):
    per = MXU_DIM // LRU_BLOCK_W
    lead = w.shape[:-3]
    wg = w.reshape(lead + (w.shape[-3] // per, per, LRU_BLOCK_W, LRU_BLOCK_W))
    eye = jnp.eye(per, dtype=w.dtype)
    bd = jnp.einsum("...gaij,ac->...gaicj", wg, eye)
    return bd.reshape(lead + (w.shape[-3] // per, MXU_DIM, MXU_DIM)).astype(BF16)


def _pad_gate_w(gate_w):
    n_layers, n_dir, rank, dk = gate_w.shape
    out = jnp.zeros((n_layers, n_dir, LANES, dk), gate_w.dtype)
    for d in range(n_dir):
        out = out.at[:, d, d * rank:(d + 1) * rank].set(gate_w[:, d])
    return out.astype(BF16)


def kernel(x, c, ctx, c_ctx, mod_w, mod_b, norm1_g, norm2_g, w_in, conv_w, conv_b, lru_a_w, lru_a_b,
           lru_x_w, lru_x_b, lru_lambda, gla_gate_w, gla_gate_b, gla_norm_g, q_norm_g, k_norm_g,
           w_branch, w_out, mlp_w1, mlp_w2):
    n_lat = x.shape[1]
    ctx_len = ctx.shape[1]
    assert x.shape[2] == D and ctx_len == SEQ_BLOCK and n_lat % SEQ_BLOCK == 0
    n_layers = mod_w.shape[0]

    xs = jnp.concatenate([ctx, x], axis=1)
    mods = _modulation(c, c_ctx, mod_w, mod_b)
    cos, sin = _rope_tables(n_lat, ctx_len)
    perm = _head_perm()
    wp = _pack_w_in(w_in)
    wa = _block_diag(lru_a_w)
    wx = _block_diag(lru_x_w)
    wg = _pad_gate_w(gla_gate_w)
    wb = w_branch.astype(BF16)
    wo = w_out.astype(BF16)
    w1 = mlp_w1.astype(BF16)
    w2 = mlp_w2.astype(BF16)

    for l in range(n_layers):
        p = _proj(xs, mods[l], norm1_g[l], wp[l], ctx_len)
        lru = lambda dr, **kw: _lru(p, conv_w[l], conv_b[l], wa[l, dr], lru_a_b[l, dr], wx[l, dr],
                                    lru_x_b[l, dr], lru_lambda[l, dr], rev=bool(dr), **kw)
        y1 = lru(1, hf=lru(0))
        gla = lambda dr, **kw: _gla(p, wg[l, dr], gla_gate_b[l, dr], rev=bool(dr), **kw)
        y2 = gla(1, of=gla(0), norm_g=gla_norm_g[l])
        qn, kn, vn = _qkv(p, cos, sin, q_norm_g[l][perm], k_norm_g[l][perm])
        q_bound = Q_SCALE_LOG2 * HEAD_DIM ** 0.5 * (1.0 + 2.0 ** -7) * jnp.max(jnp.abs(q_norm_g[l]))
        y3 = _attn(qn, kn, vn, q_bound.reshape(1, 1), ctx_len)
        x1 = _merge(xs, y1, y2, y3, p, mods[l], wb[l], wo[l], ctx_len)
        xs = _mlp(x1, mods[l], norm2_g[l], w1[l], w2[l], ctx_len)
    return xs[:, ctx_len:]
```

```python
import functools

import jax
import jax.numpy as jnp
from jax import lax
from jax.experimental import pallas as pl
from jax.experimental.pallas import tpu as pltpu

F32 = jnp.float32
BF16 = jnp.bfloat16

NORM_EPS = 1e-6
F32_TINY = 1e-37
GRID_W = 64
ROPE_THETA = 10000.0
RGLRU_C = 8.0
LRU_BLOCK_W = 64
CONV_WIDTH = 4
GLA_HEADS = 4
GLA_DK = 128
GLA_DV = 256
GLA_GATE_RANK = 16
GLA_TAU = 16.0
GLA_CHUNK = 64
ATT_HEADS = 8
ATT_KV_HEADS = 2
ATT_GROUP = ATT_HEADS // ATT_KV_HEADS
HEAD_DIM = 128
N_BRANCHES = 3
LOG2_E = 1.4426950408889634
Q_SCALE_LOG2 = HEAD_DIM ** -0.5 * LOG2_E
ATTN_SAFE_SHIFT = 40.0

LANES = 128
SUBLANES = 8
MXU_DIM = 256
VMEM_LIMIT_BYTES = 56 * 1024 * 1024

SEQ_BLOCK = 256
HALO = 2 * SUBLANES

D = 1024
P_LX, P_LY, P_GV, P_GR, P_AQ = 0, 1024, 2048, 3072, 4096
P_GQ, P_GK, P_BG = 5120, 5632, 6144
P_AK, P_AV, P_GZ = 9216, 9472, 9728
P_GZ_PAD = 256
P_WIDTH = P_GZ + P_GZ_PAD
PROJ_TN = 3328


def _first_divisor(n, candidates):
    for c in candidates:
        if n % c == 0:
            return c
    raise ValueError(f"no tile in {candidates} divides {n}")


def _params(*sem):
    return pltpu.CompilerParams(dimension_semantics=sem, vmem_limit_bytes=VMEM_LIMIT_BYTES)


def _rms(x):
    return x * lax.rsqrt(jnp.mean(x * x, axis=-1, keepdims=True) + NORM_EPS)


def _dot(a, b):
    return jnp.dot(a, b, preferred_element_type=F32)


def _dot_nt(a, b):
    return lax.dot_general(a, b, (((1,), (1,)), ((), ())), preferred_element_type=F32)


def _dot_tn(a, b):
    return lax.dot_general(a, b, (((0,), (0,)), ((), ())), preferred_element_type=F32)


def _ctx_rows(tile_idx, tm, ctx_len):
    rows = tile_idx * tm + lax.broadcasted_iota(jnp.int32, (tm, 1), 0)
    return rows < ctx_len


def _mod_kernel(c_ref, w_ref, b_ref, o_ref):
    c = c_ref[...]
    cs = (c * jax.nn.sigmoid(c)).astype(BF16)
    o_ref[...] = _dot(cs, w_ref[...].astype(BF16)) + b_ref[...]


def _modulation(c, c_ctx, mod_w, mod_b):
    n_layers, d, n6 = mod_w.shape
    b = c.shape[0]
    rows = -(-(b + 1) // SUBLANES) * SUBLANES
    cs = jnp.zeros((rows, d), F32).at[:b].set(c).at[b].set(c_ctx)
    tn = _first_divisor(n6, (1536, 1024, 512))
    out = pl.pallas_call(
        _mod_kernel,
        grid=(n_layers, n6 // tn),
        in_specs=[
            pl.BlockSpec((rows, d), lambda l, j: (0, 0)),
            pl.BlockSpec((None, d, tn), lambda l, j: (l, 0, j)),
            pl.BlockSpec((None, 1, tn), lambda l, j: (l, 0, j)),
        ],
        out_specs=pl.BlockSpec((None, rows, tn), lambda l, j: (l, 0, j)),
        out_shape=jax.ShapeDtypeStruct((n_layers, rows, n6), F32),
        compiler_params=_params("parallel", "parallel"),
        name="modulation",
    )(cs, mod_w, mod_b.reshape(n_layers, 1, n6))
    return out.reshape(n_layers, rows, 6, d)[:, :b + 1]


def _proj_kernel(x_ref, modb_ref, modc_ref, g_ref, w_ref, o_ref, h_ref, *, tm, ctx_len):
    i = pl.program_id(1)

    @pl.when(pl.program_id(2) == 0)
    def _():
        y = _rms(x_ref[...]) * g_ref[...]
        is_ctx = _ctx_rows(i, tm, ctx_len)
        shift = jnp.where(is_ctx, modc_ref[0:1, :], modb_ref[0:1, :])
        scale = jnp.where(is_ctx, modc_ref[1:2, :], modb_ref[1:2, :])
        h_ref[...] = (y * (1.0 + scale) + shift).astype(BF16)

    o_ref[...] = _dot(h_ref[...], w_ref[...]).astype(o_ref.dtype)


def _proj(xs, mods, norm_g, wp, ctx_len):
    b, s, d = xs.shape
    tm = _first_divisor(s, (1408, 1280, 640, 256))
    nb = mods.shape[0] - 1
    return pl.pallas_call(
        functools.partial(_proj_kernel, tm=tm, ctx_len=ctx_len),
        grid=(b, s // tm, P_WIDTH // PROJ_TN),
        in_specs=[
            pl.BlockSpec((None, tm, d), lambda bi, i, j: (bi, i, 0)),
            pl.BlockSpec((None, 6, d), lambda bi, i, j: (bi, 0, 0)),
            pl.BlockSpec((None, 6, d), lambda bi, i, j: (nb, 0, 0)),
            pl.BlockSpec((1, d), lambda bi, i, j: (0, 0)),
            pl.BlockSpec((d, PROJ_TN), lambda bi, i, j: (0, j)),
        ],
        out_specs=pl.BlockSpec((None, tm, PROJ_TN), lambda bi, i, j: (bi, i, j)),
        out_shape=jax.ShapeDtypeStruct((b, s, P_WIDTH), BF16),
        scratch_shapes=[pltpu.VMEM((tm, d), BF16)],
        compiler_params=_params("parallel", "parallel", "arbitrary"),
        name="proj",
    )(xs, mods, mods, norm_g.reshape(1, d), wp)


def _seq_block(tb, nblk, rev):
    if not rev:
        return tb
    return jnp.where(tb == 0, 0, nblk - tb)


def _per_sample(block_fn, shared):
    def body(*refs, **kw):
        for bb in range(refs[0].shape[0]):
            block_fn(*[r if i in shared else r.at[bb] for i, r in enumerate(refs)], **kw)
    return body


BATCH_TILE = (4, 2, 1)


def _gelu_tanh(x):
    return 0.5 * x * (1.0 + jnp.tanh(0.7978845608028654 * (x + 0.044715 * (x * x * x))))


def _segment_perm(r, inverse):
    seg = r // SUBLANES
    ri = lax.broadcasted_iota(jnp.int32, (r, r), 0)
    ci = lax.broadcasted_iota(jnp.int32, (r, r), 1)
    if inverse:
        ri, ci = ci, ri
    src = (ri & (SUBLANES - 1)) * seg + lax.shift_right_logical(ri, 3)
    return jnp.where(ci == src, 1.0, 0.0).astype(BF16)


def _sublane_scan(a, u, carry, sub, rev):
    for k in (1, 2, 4):
        if rev:
            shift, valid = SUBLANES - k, sub < SUBLANES - k
        else:
            shift, valid = k, sub >= k
        a_sh = pltpu.roll(a, shift, 0)
        u_sh = pltpu.roll(u, shift, 0)
        u = u + a * jnp.where(valid, u_sh, 0.0)
        a = a * jnp.where(valid, a_sh, 1.0)
    return a * carry + u


def _lru_kernel(*refs, rev, nblk):
    if rev:
        (lx_ref, prev_ref, next_ref, cw_ref, cb_ref, wa_ref, ab_ref, wx_ref, xb_ref, lam_ref,
         hf_ref, ly_ref, o_ref, carry_ref) = refs
    else:
        (lx_ref, prev_ref, next_ref, cw_ref, cb_ref, wa_ref, ab_ref, wx_ref, xb_ref, lam_ref,
         o_ref, carry_ref) = refs
    tb = pl.program_id(1)
    te = _seq_block(tb, nblk, rev)
    r, w = lx_ref.shape
    seg = r // SUBLANES
    top = SUBLANES - 1

    @pl.when(tb == 0)
    def _():
        carry_ref[...] = jnp.zeros_like(carry_ref)

    perm = _segment_perm(r, inverse=False)
    x = _dot(perm, lx_ref[...])
    sub = lax.broadcasted_iota(jnp.int32, (SUBLANES, w), 0)

    has_prev = te >= 2
    has_next = jnp.logical_and(te >= 1, te <= nblk - 2)
    prev_rows = jnp.where(has_prev, prev_ref[...].astype(F32), 0.0)
    next_rows = jnp.where(has_next, next_ref[...].astype(F32), 0.0)
    edge_m1 = jnp.where(sub == 0, prev_rows[HALO - 1:HALO], pltpu.roll(x[r - SUBLANES:r], 1, 0))
    edge_p1 = jnp.where(sub == top, next_rows[0:1], pltpu.roll(x[0:SUBLANES], top, 0))
    edge_p2 = jnp.where(sub == top, next_rows[1:2], pltpu.roll(x[SUBLANES:2 * SUBLANES], top, 0))
    x_m1 = jnp.concatenate([edge_m1, x[:r - SUBLANES]], axis=0)
    x_p1 = jnp.concatenate([x[SUBLANES:], edge_p1], axis=0)
    x_p2 = jnp.concatenate([x[2 * SUBLANES:], edge_p1, edge_p2], axis=0)
    xc = (cw_ref[0:1, :] * x_m1 + cw_ref[1:2, :] * x + cw_ref[2:3, :] * x_p1 + cw_ref[3:4, :] * x_p2
          + cb_ref[...])

    xcb = xc.astype(BF16)
    lam = lam_ref[...]
    neg_log_a_scale = RGLRU_C * (jnp.maximum(-lam, 0.0) + jnp.log1p(jnp.exp(-jnp.abs(lam))))
    a_parts, u_parts = [], []
    for gi in range(w // MXU_DIM):
        sl = slice(gi * MXU_DIM, (gi + 1) * MXU_DIM)
        rg = jax.nn.sigmoid(_dot(xcb[:, sl], wa_ref[gi]) + ab_ref[:, sl])
        ig = jax.nn.sigmoid(_dot(xcb[:, sl], wx_ref[gi]) + xb_ref[:, sl])
        neg_log_a = rg * neg_log_a_scale[:, sl]
        a_g = jnp.exp2(neg_log_a * -LOG2_E)
        a_parts.append(a_g)
        one_minus_a2 = jnp.tanh(neg_log_a) * (1.0 + a_g * a_g)
        root = one_minus_a2 * lax.rsqrt(jnp.maximum(one_minus_a2, F32_TINY))
        u_parts.append(root * (ig * xc[:, sl]))
    a = jnp.concatenate(a_parts, axis=1)
    u = jnp.concatenate(u_parts, axis=1)

    steps = range(seg)
    h_loc = [None] * seg
    p_cum = [None] * seg
    hl = pc = None
    for i in (reversed(steps) if rev else steps):
        rows = slice(i * SUBLANES, (i + 1) * SUBLANES)
        hl = u[rows] if hl is None else a[rows] * hl + u[rows]
        pc = a[rows] if pc is None else a[rows] * pc
        h_loc[i], p_cum[i] = hl, pc
    carry = carry_ref[...]
    h_edge = _sublane_scan(pc, hl, carry, sub, rev)
    if rev:
        seg_in = jnp.where(sub == top, carry, pltpu.roll(h_edge, top, 0))
        carry_ref[...] = h_edge[0:1, :]
    else:
        seg_in = jnp.where(sub == 0, carry, pltpu.roll(h_edge, 1, 0))
        carry_ref[...] = h_edge[top:SUBLANES, :]
    h_all = jnp.concatenate([h_loc[i] + p_cum[i] * seg_in for i in steps], axis=0)
    if rev:
        gate = _gelu_tanh(_dot(perm, ly_ref[...]))
        y = ((hf_ref[...] + h_all) * gate).astype(BF16)
        o_ref[...] = _dot(_segment_perm(r, inverse=True), y).astype(o_ref.dtype)
    else:
        o_ref[...] = h_all


def _lru(p, conv_w, conv_b, wa, ab, wx, xb, lam, rev, hf=None):
    b, s, _ = p.shape
    w = conv_w.shape[1]
    r = SEQ_BLOCK
    nblk = s // r
    per_halo = r // HALO
    last_halo = s // HALO - 1
    blk = lambda tb: _seq_block(tb, nblk, rev)
    ng = w // MXU_DIM
    bt = _first_divisor(b, BATCH_TILE)
    in_specs = [
        pl.BlockSpec((bt, r, w), lambda bi, tb: (bi, blk(tb), P_LX // w)),
        pl.BlockSpec((bt, HALO, w), lambda bi, tb: (bi, jnp.maximum(blk(tb) * per_halo - 1, 0), P_LX // w)),
        pl.BlockSpec((bt, HALO, w),
                     lambda bi, tb: (bi, jnp.minimum((blk(tb) + 1) * per_halo, last_halo), P_LX // w)),
        pl.BlockSpec((CONV_WIDTH, w), lambda bi, tb: (0, 0)),
        pl.BlockSpec((1, w), lambda bi, tb: (0, 0)),
        pl.BlockSpec((ng, MXU_DIM, MXU_DIM), lambda bi, tb: (0, 0, 0)),
        pl.BlockSpec((1, w), lambda bi, tb: (0, 0)),
        pl.BlockSpec((ng, MXU_DIM, MXU_DIM), lambda bi, tb: (0, 0, 0)),
        pl.BlockSpec((1, w), lambda bi, tb: (0, 0)),
        pl.BlockSpec((1, w), lambda bi, tb: (0, 0)),
    ]
    args = [p, p, p, conv_w, conv_b.reshape(1, w), wa, ab.reshape(1, w), wx, xb.reshape(1, w),
            lam.reshape(1, w)]
    if rev:
        in_specs += [
            pl.BlockSpec((bt, r, w), lambda bi, tb: (bi, blk(tb), 0)),
            pl.BlockSpec((bt, r, w), lambda bi, tb: (bi, blk(tb), P_LY // w)),
        ]
        args += [hf, p]
    return pl.pallas_call(
        functools.partial(_per_sample(_lru_kernel, shared=range(3, 10)), rev=rev, nblk=nblk),
        grid=(b // bt, nblk),
        in_specs=in_specs,
        out_specs=pl.BlockSpec((bt, r, w), lambda bi, tb: (bi, blk(tb), 0)),
        out_shape=jax.ShapeDtypeStruct((b, s, w), BF16 if rev else F32),
        scratch_shapes=[pltpu.VMEM((bt, 1, w), F32)],
        compiler_params=_params("arbitrary", "arbitrary"),
        name="lru_rev" if rev else "lru_fwd",
    )(*args)


def _split3(x):
    hi = x.astype(BF16)
    r1 = x - hi.astype(F32)
    mid = r1.astype(BF16)
    lo = (r1 - mid.astype(F32)).astype(BF16)
    return hi, mid, lo


def _gla_kernel(*refs, rev):
    if rev:
        q_ref, k_ref, v_ref, z_ref, wg_ref, gb_ref, of_ref, r_ref, ng_ref, o_ref, st_ref = refs
    else:
        q_ref, k_ref, v_ref, z_ref, wg_ref, gb_ref, o_ref, st_ref = refs
    r = q_ref.shape[0]
    c = GLA_CHUNK
    nch = r // c

    @pl.when(pl.program_id(1) == 0)
    def _():
        st_ref[...] = jnp.zeros_like(st_ref)

    zz = _dot(z_ref[...], wg_ref[...]) + gb_ref[...]
    g = (jnp.minimum(zz, 0.0) - jnp.log1p(jnp.exp(-jnp.abs(zz)))) * (1.0 / GLA_TAU)

    ri = lax.broadcasted_iota(jnp.int32, (r, r), 0)
    ci = lax.broadcasted_iota(jnp.int32, (r, r), 1)
    same = lax.shift_right_logical(ri, 6) == lax.shift_right_logical(ci, 6)
    tri = jnp.logical_and(same, (ci >= ri) if rev else (ci <= ri))
    tri_b = jnp.where(tri, 1.0, 0.0).astype(BF16)
    bcum = sum(_dot(tri_b, part) for part in _split3(g))

    chunks = range(nch)
    for h in range(GLA_HEADS):
        hl = slice(h * GLA_DK, (h + 1) * GLA_DK)
        vl = slice(h * GLA_DV, (h + 1) * GLA_DV)
        q = q_ref[:, hl].astype(F32) * (GLA_DK ** -0.5)
        k = k_ref[:, hl].astype(F32)
        b_h = bcum[:, hl]
        qd, kd, kl, qe, dec = [], [], [], [], []
        for ch in chunks:
            rows = slice(ch * c, (ch + 1) * c)
            b_c = b_h[rows]
            if rev:
                last, mid = b_c[0:1], b_c[c // 2:c // 2 + 1]
            else:
                last, mid = b_c[c - 1:c], b_c[c // 2 - 1:c // 2]
            qd.append(q[rows] * jnp.exp(b_c - mid))
            kd.append(k[rows] * jnp.exp(mid - b_c))
            kl.append((k[rows] * jnp.exp(last - b_c)).astype(BF16))
            qe.append((q[rows] * jnp.exp(b_c)).astype(BF16))
            dec.append(jnp.exp(last))
        qd = jnp.concatenate(qd, axis=0).astype(BF16)
        kd = jnp.concatenate(kd, axis=0).astype(BF16)
        vb = v_ref[:, vl]
        sc = jnp.where(tri, _dot_nt(qd, kd), 0.0)
        o_h = _dot(sc.astype(BF16), vb)
        st = st_ref[h]
        o_parts = [None] * nch
        for ch in (reversed(chunks) if rev else chunks):
            rows = slice(ch * c, (ch + 1) * c)
            o_parts[ch] = o_h[rows] + _dot_nt(qe[ch], st.astype(BF16))
            st = st * dec[ch] + _dot_tn(vb[rows], kl[ch])
        st_ref[h] = st
        o = jnp.concatenate(o_parts, axis=0)
        if rev:
            o = _rms(of_ref[:, vl] + o) * ng_ref[...]
            rg = r_ref[:, vl].astype(F32)
            o = o * (rg * jax.nn.sigmoid(rg))
        o_ref[:, vl] = o.astype(o_ref.dtype)


def _gla(p, wg, gb, rev, of=None, norm_g=None):
    b, s, _ = p.shape
    r = SEQ_BLOCK
    nblk = s // r
    dk = GLA_HEADS * GLA_DK
    dv = GLA_HEADS * GLA_DV
    blk = lambda tb: _seq_block(tb, nblk, rev)
    bt = _first_divisor(b, BATCH_TILE)
    in_specs = [
        pl.BlockSpec((bt, r, dk), lambda bi, tb: (bi, blk(tb), P_GQ // dk)),
        pl.BlockSpec((bt, r, dk), lambda bi, tb: (bi, blk(tb), P_GK // dk)),
        pl.BlockSpec((bt, r, dv), lambda bi, tb: (bi, blk(tb), P_GV // dv)),
        pl.BlockSpec((bt, r, LANES), lambda bi, tb: (bi, blk(tb), P_GZ // LANES)),
        pl.BlockSpec((LANES, dk), lambda bi, tb: (0, 0)),
        pl.BlockSpec((1, dk), lambda bi, tb: (0, 0)),
    ]
    args = [p, p, p, p, wg, gb.reshape(1, dk)]
    if rev:
        in_specs += [
            pl.BlockSpec((bt, r, dv), lambda bi, tb: (bi, blk(tb), 0)),
            pl.BlockSpec((bt, r, dv), lambda bi, tb: (bi, blk(tb), P_GR // dv)),
            pl.BlockSpec((1, GLA_DV), lambda bi, tb: (0, 0)),
        ]
        args += [of, p, norm_g.reshape(1, GLA_DV)]
    return pl.pallas_call(
        functools.partial(_per_sample(_gla_kernel, shared=(4, 5, 8)), rev=rev),
        grid=(b // bt, nblk),
        in_specs=in_specs,
        out_specs=pl.BlockSpec((bt, r, dv), lambda bi, tb: (bi, blk(tb), 0)),
        out_shape=jax.ShapeDtypeStruct((b, s, dv), BF16 if rev else F32),
        scratch_shapes=[pltpu.VMEM((bt, GLA_HEADS, GLA_DV, GLA_DK), F32)],
        compiler_params=_params("arbitrary", "arbitrary"),
        name="gla_rev" if rev else "gla_fwd",
    )(*args)


def _head_perm():
    quarter = HEAD_DIM // 4
    idx = jnp.arange(HEAD_DIM).reshape(4, quarter)
    return jnp.concatenate([idx[0], idx[2], idx[1], idx[3]])


def _rope_tables(n_lat, ctx_len):
    half = HEAD_DIM // 4
    t = jnp.arange(n_lat)
    inv_freq = ROPE_THETA ** (-jnp.arange(0, 2 * half, 2, dtype=F32) / (2 * half))
    ang_r = (t // GRID_W).astype(F32)[:, None] * inv_freq[None, :]
    ang_c = (t % GRID_W).astype(F32)[:, None] * inv_freq[None, :]
    cos = jnp.concatenate([jnp.cos(ang_r), jnp.cos(ang_c)] * 2, axis=-1)
    sin = jnp.concatenate([-jnp.sin(ang_r), -jnp.sin(ang_c), jnp.sin(ang_r), jnp.sin(ang_c)], axis=-1)
    cos = jnp.concatenate([jnp.ones((ctx_len, HEAD_DIM), F32), cos], axis=0)
    sin = jnp.concatenate([jnp.zeros((ctx_len, HEAD_DIM), F32), sin], axis=0)
    return cos, sin


def _qkv_kernel(aq_ref, ak_ref, av_ref, cos_ref, sin_ref, qg_ref, kg_ref, q_ref, k_ref, v_ref):
    tm = aq_ref.shape[0]
    cos, sin = cos_ref[...], sin_ref[...]

    def norm_rope(x, g):
        y = _rms(x.astype(F32)) * g
        return y * cos + pltpu.roll(y, HEAD_DIM // 2, 1) * sin

    for h in range(ATT_HEADS):
        sl = slice(h * HEAD_DIM, (h + 1) * HEAD_DIM)
        q_ref[:, sl] = (norm_rope(aq_ref[:, sl], qg_ref[...]) * Q_SCALE_LOG2).astype(BF16)
    for h in range(ATT_KV_HEADS):
        sl = slice(h * HEAD_DIM, (h + 1) * HEAD_DIM)
        k_ref[:, sl] = norm_rope(ak_ref[:, sl], kg_ref[...]).astype(BF16)
        v_ref[:, 2 * h * HEAD_DIM:(2 * h + 1) * HEAD_DIM] = av_ref[:, sl]
        v_ref[:, (2 * h + 1) * HEAD_DIM:(2 * h + 2) * HEAD_DIM] = jnp.ones((tm, HEAD_DIM), BF16)


def _qkv(p, cos, sin, q_norm_g, k_norm_g):
    b, s, _ = p.shape
    tm = _first_divisor(s, (768, 256))
    dq = ATT_HEADS * HEAD_DIM
    dkv = ATT_KV_HEADS * HEAD_DIM
    return pl.pallas_call(
        _qkv_kernel,
        grid=(b, s // tm),
        in_specs=[
            pl.BlockSpec((None, tm, dq), lambda bi, i: (bi, i, P_AQ // dq)),
            pl.BlockSpec((None, tm, dkv), lambda bi, i: (bi, i, P_AK // dkv)),
            pl.BlockSpec((None, tm, dkv), lambda bi, i: (bi, i, P_AV // dkv)),
            pl.BlockSpec((tm, HEAD_DIM), lambda bi, i: (i, 0)),
            pl.BlockSpec((tm, HEAD_DIM), lambda bi, i: (i, 0)),
            pl.BlockSpec((1, HEAD_DIM), lambda bi, i: (0, 0)),
            pl.BlockSpec((1, HEAD_DIM), lambda bi, i: (0, 0)),
        ],
        out_specs=[
            pl.BlockSpec((None, tm, dq), lambda bi, i: (bi, i, 0)),
            pl.BlockSpec((None, tm, dkv), lambda bi, i: (bi, i, 0)),
            pl.BlockSpec((None, tm, 2 * dkv), lambda bi, i: (bi, i, 0)),
        ],
        out_shape=[
            jax.ShapeDtypeStruct((b, s, dq), BF16),
            jax.ShapeDtypeStruct((b, s, dkv), BF16),
            jax.ShapeDtypeStruct((b, s, 2 * dkv), BF16),
        ],
        compiler_params=_params("parallel", "parallel"),
        name="qkv",
    )(p, p, p, cos, sin, q_norm_g.reshape(1, HEAD_DIM), k_norm_g.reshape(1, HEAD_DIM))


def _attn_scores(qs_ref, k_ref, start, size):
    return _dot_nt(qs_ref[...], k_ref[pl.ds(start, size), :])


def _attn_consume(s, v, m_ref, acc_ref):
    reps = s.shape[1] // LANES
    m_prev = m_ref[...]
    m_new = jnp.maximum(m_prev, jnp.max(s, axis=-1, keepdims=True))
    p = jnp.exp2(s - jnp.concatenate([m_new] * reps, axis=1))
    alpha = jnp.exp2(m_prev - m_new)
    acc_ref[...] = jnp.concatenate([alpha] * 2, axis=1) * acc_ref[...] + _dot(p.astype(BF16), v)
    m_ref[...] = m_new


def _attn_consume_fixed(s, v, shift, acc_ref):
    acc_ref[...] += _dot(jnp.exp2(s - shift).astype(BF16), v)


def _attn_all_keys(consume, qs_ref, k_ref, v_ref, m_ref, acc_ref, s_ref, ck):
    nck = k_ref.shape[0] // ck

    def chunk(c):
        return pl.multiple_of(c * ck, LANES)

    s_ref[...] = _attn_scores(qs_ref, k_ref, 0, ck)

    def body(i, carry):
        s_odd = _attn_scores(qs_ref, k_ref, chunk(2 * i + 1), ck)
        consume(s_ref[...], v_ref[pl.ds(chunk(2 * i), ck), :], m_ref, acc_ref)
        s_ref[...] = _attn_scores(qs_ref, k_ref, chunk(2 * i + 2), ck)
        consume(s_odd, v_ref[pl.ds(chunk(2 * i + 1), ck), :], m_ref, acc_ref)
        return carry

    pairs = (nck - 1) // 2
    lax.fori_loop(0, pairs, body, 0)
    if nck - 1 - 2 * pairs:
        s_last = _attn_scores(qs_ref, k_ref, (nck - 1) * ck, ck)
        consume(s_ref[...], v_ref[(nck - 2) * ck:(nck - 1) * ck, :], m_ref, acc_ref)
        consume(s_last, v_ref[(nck - 1) * ck:nck * ck, :], m_ref, acc_ref)
    else:
        consume(s_ref[...], v_ref[(nck - 1) * ck:nck * ck, :], m_ref, acc_ref)


def _attn_kernel(qb_ref, q_ref, k_ref, v_ref, o_ref, qs_ref, m_ref, acc_ref, s_ref, shift_ref, *, ctx_len, ck):
    tq = q_ref.shape[0]
    is_ctx_tile = pl.program_id(2) == 0
    for h in range(ATT_GROUP):
        qs_ref[h * tq:(h + 1) * tq, :] = q_ref[:, h * HEAD_DIM:(h + 1) * HEAD_DIM]
    acc_ref[...] = jnp.zeros_like(acc_ref)

    @pl.when(is_ctx_tile)
    def _():
        kf = k_ref[...].astype(F32)
        shift_ref[0] = qb_ref[0, 0] * jnp.max(jnp.sqrt(jnp.sum(kf * kf, axis=-1, keepdims=True)))
        m_ref[...] = jnp.full_like(m_ref, -jnp.inf)
        _attn_consume(_attn_scores(qs_ref, k_ref, 0, ctx_len), v_ref[0:ctx_len, :], m_ref, acc_ref)

    @pl.when(jnp.logical_not(is_ctx_tile))
    def _():
        shift = shift_ref[0]
        bound_is_safe = shift <= ATTN_SAFE_SHIFT

        @pl.when(bound_is_safe)
        def _():
            _attn_all_keys(_attn_consume_fixed, qs_ref, k_ref, v_ref, shift, acc_ref, s_ref, ck)

        @pl.when(jnp.logical_not(bound_is_safe))
        def _():
            m_ref[...] = jnp.full_like(m_ref, -jnp.inf)
            _attn_all_keys(_attn_consume, qs_ref, k_ref, v_ref, m_ref, acc_ref, s_ref, ck)

    acc = acc_ref[...]
    out = acc[:, :HEAD_DIM] / acc[:, HEAD_DIM:]
    for h in range(ATT_GROUP):
        o_ref[:, h * HEAD_DIM:(h + 1) * HEAD_DIM] = out[h * tq:(h + 1) * tq].astype(o_ref.dtype)


def _attn(qn, kn, vn, q_bound, ctx_len):
    b, s, dq = qn.shape
    tq = SEQ_BLOCK
    ck = _first_divisor(s, (768, 640, 256))
    gw = ATT_GROUP * HEAD_DIM
    rows = ATT_GROUP * tq
    return pl.pallas_call(
        functools.partial(_attn_kernel, ctx_len=ctx_len, ck=ck),
        grid=(b, ATT_KV_HEADS, s // tq),
        in_specs=[
            pl.BlockSpec(memory_space=pltpu.SMEM),
            pl.BlockSpec((None, tq, gw), lambda bi, h, i: (bi, i, h)),
            pl.BlockSpec((None, s, HEAD_DIM), lambda bi, h, i: (bi, 0, h)),
            pl.BlockSpec((None, s, 2 * HEAD_DIM), lambda bi, h, i: (bi, 0, h)),
        ],
        out_specs=pl.BlockSpec((None, tq, gw), lambda bi, h, i: (bi, i, h)),
        out_shape=jax.ShapeDtypeStruct((b, s, dq), BF16),
        scratch_shapes=[
            pltpu.VMEM((rows, HEAD_DIM), BF16),
            pltpu.VMEM((rows, LANES), F32),
            pltpu.VMEM((rows, 2 * HEAD_DIM), F32),
            pltpu.VMEM((rows, ck), F32),
            pltpu.SMEM((1,), F32),
        ],
        compiler_params=_params("parallel", "parallel", "arbitrary"),
        name="attn",
    )(q_bound, qn, kn, vn)


def _merge_kernel(x_ref, y1_ref, y2_ref, y3_ref, bg_ref, modb_ref, modc_ref, wb_ref, wo_ref, o_ref,
                  *, tm, ctx_len):
    d = x_ref.shape[1]
    gate = jax.nn.sigmoid(bg_ref[...].astype(F32))
    m = (gate[:, 0:d] * _dot(y1_ref[...], wb_ref[0])
         + gate[:, d:2 * d] * _dot(y2_ref[...], wb_ref[1])
         + gate[:, 2 * d:3 * d] * _dot(y3_ref[...], wb_ref[2]))
    upd = _dot(m.astype(BF16), wo_ref[...])
    is_ctx = _ctx_rows(pl.program_id(1), tm, ctx_len)
    ga = jnp.where(is_ctx, modc_ref[2:3, :], modb_ref[2:3, :])
    o_ref[...] = x_ref[...] + ga * upd


def _merge(xs, y1, y2, y3, p, mods, wb, wo, ctx_len):
    b, s, d = xs.shape
    tm = _first_divisor(s, (768, 640, 256))
    nb = mods.shape[0] - 1
    row = lambda bi, i: (bi, i, 0)
    return pl.pallas_call(
        functools.partial(_merge_kernel, tm=tm, ctx_len=ctx_len),
        grid=(b, s // tm),
        in_specs=[
            pl.BlockSpec((None, tm, d), row),
            pl.BlockSpec((None, tm, d), row),
            pl.BlockSpec((None, tm, d), row),
            pl.BlockSpec((None, tm, d), row),
            pl.BlockSpec((None, tm, N_BRANCHES * d), lambda bi, i: (bi, i, P_BG // (N_BRANCHES * d))),
            pl.BlockSpec((None, 6, d), lambda bi, i: (bi, 0, 0)),
            pl.BlockSpec((None, 6, d), lambda bi, i: (nb, 0, 0)),
            pl.BlockSpec((N_BRANCHES, d, d), lambda bi, i: (0, 0, 0)),
            pl.BlockSpec((d, d), lambda bi, i: (0, 0)),
        ],
        out_specs=pl.BlockSpec((None, tm, d), row),
        out_shape=jax.ShapeDtypeStruct((b, s, d), F32),
        compiler_params=_params("parallel", "parallel"),
        name="merge",
    )(xs, y1, y2, y3, p, mods, mods, wb, wo)


def _mlp_kernel(x_ref, modb_ref, modc_ref, g_ref, w1_ref, w2_ref, o_ref, h_ref, acc_ref, *, tm, ctx_len):
    i = pl.program_id(1)
    j = pl.program_id(2)
    is_ctx = _ctx_rows(i, tm, ctx_len)

    @pl.when(j == 0)
    def _():
        y = _rms(x_ref[...]) * g_ref[...]
        shift = jnp.where(is_ctx, modc_ref[3:4, :], modb_ref[3:4, :])
        scale = jnp.where(is_ctx, modc_ref[4:5, :], modb_ref[4:5, :])
        h_ref[...] = (y * (1.0 + scale) + shift).astype(BF16)
        acc_ref[...] = jnp.zeros_like(acc_ref)

    t = jnp.maximum(_dot(h_ref[...], w1_ref[...]), 0.0)
    acc_ref[...] += _dot((t * t).astype(BF16), w2_ref[...])

    @pl.when(j == pl.num_programs(2) - 1)
    def _():
        ga = jnp.where(is_ctx, modc_ref[5:6, :], modb_ref[5:6, :])
        o_ref[...] = x_ref[...] + ga * acc_ref[...]


def _mlp(xs, mods, norm_g, w1, w2, ctx_len):
    b, s, d = xs.shape
    dff = w1.shape[1]
    tm = _first_divisor(s, (1408, 1280, 640, 256))
    tf = _first_divisor(dff, (1024, 512))
    nb = mods.shape[0] - 1
    return pl.pallas_call(
        functools.partial(_mlp_kernel, tm=tm, ctx_len=ctx_len),
        grid=(b, s // tm, dff // tf),
        in_specs=[
            pl.BlockSpec((None, tm, d), lambda bi, i, j: (bi, i, 0)),
            pl.BlockSpec((None, 6, d), lambda bi, i, j: (bi, 0, 0)),
            pl.BlockSpec((None, 6, d), lambda bi, i, j: (nb, 0, 0)),
            pl.BlockSpec((1, d), lambda bi, i, j: (0, 0)),
            pl.BlockSpec((d, tf), lambda bi, i, j: (0, j)),
            pl.BlockSpec((tf, d), lambda bi, i, j: (j, 0)),
        ],
        out_specs=pl.BlockSpec((None, tm, d), lambda bi, i, j: (bi, i, 0)),
        out_shape=jax.ShapeDtypeStruct((b, s, d), F32),
        scratch_shapes=[pltpu.VMEM((tm, d), BF16), pltpu.VMEM((tm, d), F32)],
        compiler_params=_params("parallel", "parallel", "arbitrary"),
        name="mlp",
    )(xs, mods, mods, norm_g.reshape(1, d), w1, w2)


def _pack_w_in(w_in):
    lx, ly, gq, gk, gv, gr, gz, aq, ak, av, bg = jnp.split(
        w_in.astype(BF16), (1024, 2048, 2560, 3072, 4096, 5120, 5152, 6176, 6432, 6688), axis=-1)
    pad = jnp.zeros(w_in.shape[:-1] + (P_GZ_PAD - gz.shape[-1],), BF16)

    def permute_heads(w):
        q4 = w.reshape(w.shape[:-1] + (w.shape[-1] // HEAD_DIM, 4, HEAD_DIM // 4))
        return jnp.stack([q4[..., 0, :], q4[..., 2, :], q4[..., 1, :], q4[..., 3, :]], axis=-2).reshape(w.shape)

    return jnp.concatenate([lx, ly, gv, gr, permute_heads(aq), gq, gk, bg, permute_heads(ak), av, gz, pad],
                           axis=-1)


def _block_diag(w):
    per = MXU_DIM // LRU_BLOCK_W
    lead = w.shape[:-3]
    wg = w.reshape(lead + (w.shape[-3] // per, per, LRU_BLOCK_W, LRU_BLOCK_W))
    eye = jnp.eye(per, dtype=w.dtype)
    bd = jnp.einsum("...gaij,ac->...gaicj", wg, eye)
    return bd.reshape(lead + (w.shape[-3] // per, MXU_DIM, MXU_DIM)).astype(BF16)


def _pad_gate_w(gate_w):
    n_layers, n_dir, rank, dk = gate_w.shape
    out = jnp.zeros((n_layers, n_dir, LANES, dk), gate_w.dtype)
    for d in range(n_dir):
        out = out.at[:, d, d * rank:(d + 1) * rank].set(gate_w[:, d])
    return out.astype(BF16)


def kernel(x, c, ctx, c_ctx, mod_w, mod_b, norm1_g, norm2_g, w_in, conv_w, conv_b, lru_a_w, lru_a_b,
           lru_x_w, lru_x_b, lru_lambda, gla_gate_w, gla_gate_b, gla_norm_g, q_norm_g, k_norm_g,
           w_branch, w_out, mlp_w1, mlp_w2):
    n_lat = x.shape[1]
    ctx_len = ctx.shape[1]
    assert x.shape[2] == D and ctx_len == SEQ_BLOCK and n_lat % SEQ_BLOCK == 0
    n_layers = mod_w.shape[0]

    xs = jnp.concatenate([ctx, x], axis=1)
    mods = _modulation(c, c_ctx, mod_w, mod_b)
    cos, sin = _rope_tables(n_lat, ctx_len)
    perm = _head_perm()
    wp = _pack_w_in(w_in)
    wa = _block_diag(lru_a_w)
    wx = _block_diag(lru_x_w)
    wg = _pad_gate_w(gla_gate_w)
    wb = w_branch.astype(BF16)
    wo = w_out.astype(BF16)
    w1 = mlp_w1.astype(BF16)
    w2 = mlp_w2.astype(BF16)

    for l in range(n_layers):
        p = _proj(xs, mods[l], norm1_g[l], wp[l], ctx_len)
        lru = lambda dr, **kw: _lru(p, conv_w[l], conv_b[l], wa[l, dr], lru_a_b[l, dr], wx[l, dr],
                                    lru_x_b[l, dr], lru_lambda[l, dr], rev=bool(dr), **kw)
        y1 = lru(1, hf=lru(0))
        gla = lambda dr, **kw: _gla(p, wg[l, dr], gla_gate_b[l, dr], rev=bool(dr), **kw)
        y2 = gla(1, of=gla(0), norm_g=gla_norm_g[l])
        qn, kn, vn = _qkv(p, cos, sin, q_norm_g[l][perm], k_norm_g[l][perm])
        q_bound = Q_SCALE_LOG2 * HEAD_DIM ** 0.5 * (1.0 + 2.0 ** -7) * jnp.max(jnp.abs(q_norm_g[l]))
        y3 = _attn(qn, kn, vn, q_bound.reshape(1, 1), ctx_len)
        x1 = _merge(xs, y1, y2, y3, p, mods[l], wb[l], wo[l], ctx_len)
        xs = _mlp(x1, mods[l], norm2_g[l], w1[l], w2[l], ctx_len)
    return xs[:, ctx_len:]
```

```python
import functools

import jax
import jax.numpy as jnp
from jax import lax
from jax.experimental import pallas as pl
from jax.experimental.pallas import tpu as pltpu

F32 = jnp.float32
BF16 = jnp.bfloat16

NORM_EPS = 1e-6
F32_TINY = 1e-37
GRID_W = 64
ROPE_THETA = 10000.0
RGLRU_C = 8.0
LRU_BLOCK_W = 64
CONV_WIDTH = 4
GLA_HEADS = 4
GLA_DK = 128
GLA_DV = 256
GLA_GATE_RANK = 16
GLA_TAU = 16.0
GLA_CHUNK = 64
ATT_HEADS = 8
ATT_KV_HEADS = 2
ATT_GROUP = ATT_HEADS // ATT_KV_HEADS
HEAD_DIM = 128
N_BRANCHES = 3
LOG2_E = 1.4426950408889634
Q_SCALE_LOG2 = HEAD_DIM ** -0.5 * LOG2_E
ATTN_SAFE_SHIFT = 40.0

LANES = 128
SUBLANES = 8
MXU_DIM = 256
ATTN_CHUNK = 3 * MXU_DIM
VMEM_LIMIT_BYTES = 56 * 1024 * 1024

SEQ_BLOCK = 256
HALO = 2 * SUBLANES

D = 1024
P_LX, P_LY, P_GV, P_GR, P_AQ = 0, 1024, 2048, 3072, 4096
P_GQ, P_GK, P_BG = 5120, 5632, 6144
P_AK, P_AV, P_GZ = 9216, 9472, 9728
P_GZ_PAD = 256
P_WIDTH = P_GZ + P_GZ_PAD
PROJ_TN = 3328


def _first_divisor(n, candidates):
    for c in candidates:
        if n % c == 0:
            return c
    raise ValueError(f"no tile in {candidates} divides {n}")


def _params(*sem):
    return pltpu.CompilerParams(dimension_semantics=sem, vmem_limit_bytes=VMEM_LIMIT_BYTES)


def _rms(x):
    return x * lax.rsqrt(jnp.mean(x * x, axis=-1, keepdims=True) + NORM_EPS)


def _dot(a, b):
    return jnp.dot(a, b, preferred_element_type=F32)


def _dot_nt(a, b):
    return lax.dot_general(a, b, (((1,), (1,)), ((), ())), preferred_element_type=F32)


def _dot_tn(a, b):
    return lax.dot_general(a, b, (((0,), (0,)), ((), ())), preferred_element_type=F32)


def _ctx_rows(tile_idx, tm, ctx_len):
    rows = tile_idx * tm + lax.broadcasted_iota(jnp.int32, (tm, 1), 0)
    return rows < ctx_len


def _mod_kernel(c_ref, w_ref, b_ref, o_ref):
    c = c_ref[...]
    cs = (c * jax.nn.sigmoid(c)).astype(BF16)
    o_ref[...] = _dot(cs, w_ref[...].astype(BF16)) + b_ref[...]


def _modulation(c, c_ctx, mod_w, mod_b):
    n_layers, d, n6 = mod_w.shape
    b = c.shape[0]
    rows = -(-(b + 1) // SUBLANES) * SUBLANES
    cs = jnp.zeros((rows, d), F32).at[:b].set(c).at[b].set(c_ctx)
    tn = _first_divisor(n6, (1536, 1024, 512))
    out = pl.pallas_call(
        _mod_kernel,
        grid=(n_layers, n6 // tn),
        in_specs=[
            pl.BlockSpec((rows, d), lambda l, j: (0, 0)),
            pl.BlockSpec((None, d, tn), lambda l, j: (l, 0, j)),
            pl.BlockSpec((None, 1, tn), lambda l, j: (l, 0, j)),
        ],
        out_specs=pl.BlockSpec((None, rows, tn), lambda l, j: (l, 0, j)),
        out_shape=jax.ShapeDtypeStruct((n_layers, rows, n6), F32),
        compiler_params=_params("parallel", "parallel"),
        name="modulation",
    )(cs, mod_w, mod_b.reshape(n_layers, 1, n6))
    return out.reshape(n_layers, rows, 6, d)[:, :b + 1]


def _proj_kernel(x_ref, modb_ref, modc_ref, g_ref, w_ref, o_ref, h_ref, *, tm, ctx_len):
    i = pl.program_id(1)

    @pl.when(pl.program_id(2) == 0)
    def _():
        y = _rms(x_ref[...]) * g_ref[...]
        is_ctx = _ctx_rows(i, tm, ctx_len)
        shift = jnp.where(is_ctx, modc_ref[0:1, :], modb_ref[0:1, :])
        scale = jnp.where(is_ctx, modc_ref[1:2, :], modb_ref[1:2, :])
        h_ref[...] = (y * (1.0 + scale) + shift).astype(BF16)

    o_ref[...] = _dot(h_ref[...], w_ref[...]).astype(o_ref.dtype)


def _proj(xs, mods, norm_g, wp, ctx_len):
    b, s, d = xs.shape
    tm = _first_divisor(s, (1408, 1280, 640, 256))
    nb = mods.shape[0] - 1
    return pl.pallas_call(
        functools.partial(_proj_kernel, tm=tm, ctx_len=ctx_len),
        grid=(b, s // tm, P_WIDTH // PROJ_TN),
        in_specs=[
            pl.BlockSpec((None, tm, d), lambda bi, i, j: (bi, i, 0)),
            pl.BlockSpec((None, 6, d), lambda bi, i, j: (bi, 0, 0)),
            pl.BlockSpec((None, 6, d), lambda bi, i, j: (nb, 0, 0)),
            pl.BlockSpec((1, d), lambda bi, i, j: (0, 0)),
            pl.BlockSpec((d, PROJ_TN), lambda bi, i, j: (0, j)),
        ],
        out_specs=pl.BlockSpec((None, tm, PROJ_TN), lambda bi, i, j: (bi, i, j)),
        out_shape=jax.ShapeDtypeStruct((b, s, P_WIDTH), BF16),
        scratch_shapes=[pltpu.VMEM((tm, d), BF16)],
        compiler_params=_params("parallel", "parallel", "arbitrary"),
        name="proj",
    )(xs, mods, mods, norm_g.reshape(1, d), wp)


def _seq_block(tb, nblk, rev):
    if not rev:
        return tb
    return jnp.where(tb == 0, 0, nblk - tb)


def _per_sample(block_fn, shared):
    def body(*refs, **kw):
        for bb in range(refs[0].shape[0]):
            block_fn(*[r if i in shared else r.at[bb] for i, r in enumerate(refs)], **kw)
    return body


BATCH_TILE = (4, 2, 1)


def _gelu_tanh(x):
    return 0.5 * x * (1.0 + jnp.tanh(0.7978845608028654 * (x + 0.044715 * (x * x * x))))


def _segment_perm(r, inverse):
    seg = r // SUBLANES
    ri = lax.broadcasted_iota(jnp.int32, (r, r), 0)
    ci = lax.broadcasted_iota(jnp.int32, (r, r), 1)
    if inverse:
        ri, ci = ci, ri
    src = (ri & (SUBLANES - 1)) * seg + lax.shift_right_logical(ri, 3)
    return jnp.where(ci == src, 1.0, 0.0).astype(BF16)


def _sublane_scan(a, u, carry, sub, rev):
    for k in (1, 2, 4):
        if rev:
            shift, valid = SUBLANES - k, sub < SUBLANES - k
        else:
            shift, valid = k, sub >= k
        a_sh = pltpu.roll(a, shift, 0)
        u_sh = pltpu.roll(u, shift, 0)
        u = u + a * jnp.where(valid, u_sh, 0.0)
        a = a * jnp.where(valid, a_sh, 1.0)
    return a * carry + u


def _lru_kernel(*refs, rev, nblk):
    if rev:
        (lx_ref, prev_ref, next_ref, cw_ref, cb_ref, wa_ref, ab_ref, wx_ref, xb_ref, lam_ref,
         hf_ref, ly_ref, o_ref, carry_ref) = refs
    else:
        (lx_ref, prev_ref, next_ref, cw_ref, cb_ref, wa_ref, ab_ref, wx_ref, xb_ref, lam_ref,
         o_ref, carry_ref) = refs
    tb = pl.program_id(1)
    te = _seq_block(tb, nblk, rev)
    r, w = lx_ref.shape
    seg = r // SUBLANES
    top = SUBLANES - 1

    @pl.when(tb == 0)
    def _():
        carry_ref[...] = jnp.zeros_like(carry_ref)

    perm = _segment_perm(r, inverse=False)
    x = _dot(perm, lx_ref[...])
    sub = lax.broadcasted_iota(jnp.int32, (SUBLANES, w), 0)

    has_prev = te >= 2
    has_next = jnp.logical_and(te >= 1, te <= nblk - 2)
    prev_rows = jnp.where(has_prev, prev_ref[...].astype(F32), 0.0)
    next_rows = jnp.where(has_next, next_ref[...].astype(F32), 0.0)
    edge_m1 = jnp.where(sub == 0, prev_rows[HALO - 1:HALO], pltpu.roll(x[r - SUBLANES:r], 1, 0))
    edge_p1 = jnp.where(sub == top, next_rows[0:1], pltpu.roll(x[0:SUBLANES], top, 0))
    edge_p2 = jnp.where(sub == top, next_rows[1:2], pltpu.roll(x[SUBLANES:2 * SUBLANES], top, 0))
    x_m1 = jnp.concatenate([edge_m1, x[:r - SUBLANES]], axis=0)
    x_p1 = jnp.concatenate([x[SUBLANES:], edge_p1], axis=0)
    x_p2 = jnp.concatenate([x[2 * SUBLANES:], edge_p1, edge_p2], axis=0)
    xc = (cw_ref[0:1, :] * x_m1 + cw_ref[1:2, :] * x + cw_ref[2:3, :] * x_p1 + cw_ref[3:4, :] * x_p2
          + cb_ref[...])

    xcb = xc.astype(BF16)
    lam = lam_ref[...]
    neg_log_a_scale = RGLRU_C * (jnp.maximum(-lam, 0.0) + jnp.log1p(jnp.exp(-jnp.abs(lam))))
    a_parts, u_parts = [], []
    for gi in range(w // MXU_DIM):
        sl = slice(gi * MXU_DIM, (gi + 1) * MXU_DIM)
        rg = jax.nn.sigmoid(_dot(xcb[:, sl], wa_ref[gi]) + ab_ref[:, sl])
        ig = jax.nn.sigmoid(_dot(xcb[:, sl], wx_ref[gi]) + xb_ref[:, sl])
        neg_log_a = rg * neg_log_a_scale[:, sl]
        a_g = jnp.exp2(neg_log_a * -LOG2_E)
        a_parts.append(a_g)
        one_minus_a2 = jnp.tanh(neg_log_a) * (1.0 + a_g * a_g)
        root = one_minus_a2 * lax.rsqrt(jnp.maximum(one_minus_a2, F32_TINY))
        u_parts.append(root * (ig * xc[:, sl]))
    a = jnp.concatenate(a_parts, axis=1)
    u = jnp.concatenate(u_parts, axis=1)

    steps = range(seg)
    h_loc = [None] * seg
    p_cum = [None] * seg
    hl = pc = None
    for i in (reversed(steps) if rev else steps):
        rows = slice(i * SUBLANES, (i + 1) * SUBLANES)
        hl = u[rows] if hl is None else a[rows] * hl + u[rows]
        pc = a[rows] if pc is None else a[rows] * pc
        h_loc[i], p_cum[i] = hl, pc
    carry = carry_ref[...]
    h_edge = _sublane_scan(pc, hl, carry, sub, rev)
    if rev:
        seg_in = jnp.where(sub == top, carry, pltpu.roll(h_edge, top, 0))
        carry_ref[...] = h_edge[0:1, :]
    else:
        seg_in = jnp.where(sub == 0, carry, pltpu.roll(h_edge, 1, 0))
        carry_ref[...] = h_edge[top:SUBLANES, :]
    h_all = jnp.concatenate([h_loc[i] + p_cum[i] * seg_in for i in steps], axis=0)
    if rev:
        gate = _gelu_tanh(_dot(perm, ly_ref[...]))
        y = ((hf_ref[...] + h_all) * gate).astype(BF16)
        o_ref[...] = _dot(_segment_perm(r, inverse=True), y).astype(o_ref.dtype)
    else:
        o_ref[...] = h_all


def _lru(p, conv_w, conv_b, wa, ab, wx, xb, lam, rev, hf=None):
    b, s, _ = p.shape
    w = conv_w.shape[1]
    r = SEQ_BLOCK
    nblk = s // r
    per_halo = r // HALO
    last_halo = s // HALO - 1
    blk = lambda tb: _seq_block(tb, nblk, rev)
    ng = w // MXU_DIM
    bt = _first_divisor(b, BATCH_TILE)
    in_specs = [
        pl.BlockSpec((bt, r, w), lambda bi, tb: (bi, blk(tb), P_LX // w)),
        pl.BlockSpec((bt, HALO, w), lambda bi, tb: (bi, jnp.maximum(blk(tb) * per_halo - 1, 0), P_LX // w)),
        pl.BlockSpec((bt, HALO, w),
                     lambda bi, tb: (bi, jnp.minimum((blk(tb) + 1) * per_halo, last_halo), P_LX // w)),
        pl.BlockSpec((CONV_WIDTH, w), lambda bi, tb: (0, 0)),
        pl.BlockSpec((1, w), lambda bi, tb: (0, 0)),
        pl.BlockSpec((ng, MXU_DIM, MXU_DIM), lambda bi, tb: (0, 0, 0)),
        pl.BlockSpec((1, w), lambda bi, tb: (0, 0)),
        pl.BlockSpec((ng, MXU_DIM, MXU_DIM), lambda bi, tb: (0, 0, 0)),
        pl.BlockSpec((1, w), lambda bi, tb: (0, 0)),
        pl.BlockSpec((1, w), lambda bi, tb: (0, 0)),
    ]
    args = [p, p, p, conv_w, conv_b.reshape(1, w), wa, ab.reshape(1, w), wx, xb.reshape(1, w),
            lam.reshape(1, w)]
    if rev:
        in_specs += [
            pl.BlockSpec((bt, r, w), lambda bi, tb: (bi, blk(tb), 0)),
            pl.BlockSpec((bt, r, w), lambda bi, tb: (bi, blk(tb), P_LY // w)),
        ]
        args += [hf, p]
    return pl.pallas_call(
        functools.partial(_per_sample(_lru_kernel, shared=range(3, 10)), rev=rev, nblk=nblk),
        grid=(b // bt, nblk),
        in_specs=in_specs,
        out_specs=pl.BlockSpec((bt, r, w), lambda bi, tb: (bi, blk(tb), 0)),
        out_shape=jax.ShapeDtypeStruct((b, s, w), BF16 if rev else F32),
        scratch_shapes=[pltpu.VMEM((bt, 1, w), F32)],
        compiler_params=_params("arbitrary", "arbitrary"),
        name="lru_rev" if rev else "lru_fwd",
    )(*args)


def _split3(x):
    hi = x.astype(BF16)
    r1 = x - hi.astype(F32)
    mid = r1.astype(BF16)
    lo = (r1 - mid.astype(F32)).astype(BF16)
    return hi, mid, lo


def _gla_kernel(*refs, rev):
    if rev:
        q_ref, k_ref, v_ref, z_ref, wg_ref, gb_ref, of_ref, r_ref, ng_ref, o_ref, st_ref = refs
    else:
        q_ref, k_ref, v_ref, z_ref, wg_ref, gb_ref, o_ref, st_ref = refs
    r = q_ref.shape[0]
    c = GLA_CHUNK
    nch = r // c

    @pl.when(pl.program_id(1) == 0)
    def _():
        st_ref[...] = jnp.zeros_like(st_ref)

    zz = _dot(z_ref[...], wg_ref[...]) + gb_ref[...]
    g = (jnp.minimum(zz, 0.0) - jnp.log1p(jnp.exp(-jnp.abs(zz)))) * (1.0 / GLA_TAU)

    ri = lax.broadcasted_iota(jnp.int32, (r, r), 0)
    ci = lax.broadcasted_iota(jnp.int32, (r, r), 1)
    same = lax.shift_right_logical(ri, 6) == lax.shift_right_logical(ci, 6)
    tri = jnp.logical_and(same, (ci >= ri) if rev else (ci <= ri))
    tri_b = jnp.where(tri, 1.0, 0.0).astype(BF16)
    bcum = sum(_dot(tri_b, part) for part in _split3(g))

    chunks = range(nch)
    for h in range(GLA_HEADS):
        hl = slice(h * GLA_DK, (h + 1) * GLA_DK)
        vl = slice(h * GLA_DV, (h + 1) * GLA_DV)
        q = q_ref[:, hl].astype(F32) * (GLA_DK ** -0.5)
        k = k_ref[:, hl].astype(F32)
        b_h = bcum[:, hl]
        qd, kd, kl, qe, dec = [], [], [], [], []
        for ch in chunks:
            rows = slice(ch * c, (ch + 1) * c)
            b_c = b_h[rows]
            if rev:
                last, mid = b_c[0:1], b_c[c // 2:c // 2 + 1]
            else:
                last, mid = b_c[c - 1:c], b_c[c // 2 - 1:c // 2]
            qd.append(q[rows] * jnp.exp(b_c - mid))
            kd.append(k[rows] * jnp.exp(mid - b_c))
            kl.append((k[rows] * jnp.exp(last - b_c)).astype(BF16))
            qe.append((q[rows] * jnp.exp(b_c)).astype(BF16))
            dec.append(jnp.exp(last))
        qd = jnp.concatenate(qd, axis=0).astype(BF16)
        kd = jnp.concatenate(kd, axis=0).astype(BF16)
        vb = v_ref[:, vl]
        sc = jnp.where(tri, _dot_nt(qd, kd), 0.0)
        o_h = _dot(sc.astype(BF16), vb)
        st = st_ref[h]
        o_parts = [None] * nch
        for ch in (reversed(chunks) if rev else chunks):
            rows = slice(ch * c, (ch + 1) * c)
            o_parts[ch] = o_h[rows] + _dot_nt(qe[ch], st.astype(BF16))
            st = st * dec[ch] + _dot_tn(vb[rows], kl[ch])
        st_ref[h] = st
        o = jnp.concatenate(o_parts, axis=0)
        if rev:
            o = _rms(of_ref[:, vl] + o) * ng_ref[...]
            rg = r_ref[:, vl].astype(F32)
            o = o * (rg * jax.nn.sigmoid(rg))
        o_ref[:, vl] = o.astype(o_ref.dtype)


def _gla(p, wg, gb, rev, of=None, norm_g=None):
    b, s, _ = p.shape
    r = SEQ_BLOCK
    nblk = s // r
    dk = GLA_HEADS * GLA_DK
    dv = GLA_HEADS * GLA_DV
    blk = lambda tb: _seq_block(tb, nblk, rev)
    bt = _first_divisor(b, BATCH_TILE)
    in_specs = [
        pl.BlockSpec((bt, r, dk), lambda bi, tb: (bi, blk(tb), P_GQ // dk)),
        pl.BlockSpec((bt, r, dk), lambda bi, tb: (bi, blk(tb), P_GK // dk)),
        pl.BlockSpec((bt, r, dv), lambda bi, tb: (bi, blk(tb), P_GV // dv)),
        pl.BlockSpec((bt, r, LANES), lambda bi, tb: (bi, blk(tb), P_GZ // LANES)),
        pl.BlockSpec((LANES, dk), lambda bi, tb: (0, 0)),
        pl.BlockSpec((1, dk), lambda bi, tb: (0, 0)),
    ]
    args = [p, p, p, p, wg, gb.reshape(1, dk)]
    if rev:
        in_specs += [
            pl.BlockSpec((bt, r, dv), lambda bi, tb: (bi, blk(tb), 0)),
            pl.BlockSpec((bt, r, dv), lambda bi, tb: (bi, blk(tb), P_GR // dv)),
            pl.BlockSpec((1, GLA_DV), lambda bi, tb: (0, 0)),
        ]
        args += [of, p, norm_g.reshape(1, GLA_DV)]
    return pl.pallas_call(
        functools.partial(_per_sample(_gla_kernel, shared=(4, 5, 8)), rev=rev),
        grid=(b // bt, nblk),
        in_specs=in_specs,
        out_specs=pl.BlockSpec((bt, r, dv), lambda bi, tb: (bi, blk(tb), 0)),
        out_shape=jax.ShapeDtypeStruct((b, s, dv), BF16 if rev else F32),
        scratch_shapes=[pltpu.VMEM((bt, GLA_HEADS, GLA_DV, GLA_DK), F32)],
        compiler_params=_params("arbitrary", "arbitrary"),
        name="gla_rev" if rev else "gla_fwd",
    )(*args)


def _head_perm():
    quarter = HEAD_DIM // 4
    idx = jnp.arange(HEAD_DIM).reshape(4, quarter)
    return jnp.concatenate([idx[0], idx[2], idx[1], idx[3]])


def _rope_tables(n_lat, ctx_len):
    half = HEAD_DIM // 4
    t = jnp.arange(n_lat)
    inv_freq = ROPE_THETA ** (-jnp.arange(0, 2 * half, 2, dtype=F32) / (2 * half))
    ang_r = (t // GRID_W).astype(F32)[:, None] * inv_freq[None, :]
    ang_c = (t % GRID_W).astype(F32)[:, None] * inv_freq[None, :]
    cos = jnp.concatenate([jnp.cos(ang_r), jnp.cos(ang_c)] * 2, axis=-1)
    sin = jnp.concatenate([-jnp.sin(ang_r), -jnp.sin(ang_c), jnp.sin(ang_r), jnp.sin(ang_c)], axis=-1)
    cos = jnp.concatenate([jnp.ones((ctx_len, HEAD_DIM), F32), cos], axis=0)
    sin = jnp.concatenate([jnp.zeros((ctx_len, HEAD_DIM), F32), sin], axis=0)
    return cos, sin


def _qkv_kernel(aq_ref, ak_ref, av_ref, cos_ref, sin_ref, qg_ref, kg_ref, q_ref, k_ref, v_ref):
    tm = aq_ref.shape[0]
    cos, sin = cos_ref[...], sin_ref[...]

    def norm_rope(x, g):
        y = _rms(x.astype(F32)) * g
        return y * cos + pltpu.roll(y, HEAD_DIM // 2, 1) * sin

    for h in range(ATT_HEADS):
        sl = slice(h * HEAD_DIM, (h + 1) * HEAD_DIM)
        q_ref[:, sl] = (norm_rope(aq_ref[:, sl], qg_ref[...]) * Q_SCALE_LOG2).astype(BF16)
    for h in range(ATT_KV_HEADS):
        sl = slice(h * HEAD_DIM, (h + 1) * HEAD_DIM)
        k_ref[:, sl] = norm_rope(ak_ref[:, sl], kg_ref[...]).astype(BF16)
        v_ref[:, 2 * h * HEAD_DIM:(2 * h + 1) * HEAD_DIM] = av_ref[:, sl]
        v_ref[:, (2 * h + 1) * HEAD_DIM:(2 * h + 2) * HEAD_DIM] = jnp.ones((tm, HEAD_DIM), BF16)


def _qkv(p, cos, sin, q_norm_g, k_norm_g):
    b, s, _ = p.shape
    tm = _first_divisor(s, (768, 256))
    dq = ATT_HEADS * HEAD_DIM
    dkv = ATT_KV_HEADS * HEAD_DIM
    return pl.pallas_call(
        _qkv_kernel,
        grid=(b, s // tm),
        in_specs=[
            pl.BlockSpec((None, tm, dq), lambda bi, i: (bi, i, P_AQ // dq)),
            pl.BlockSpec((None, tm, dkv), lambda bi, i: (bi, i, P_AK // dkv)),
            pl.BlockSpec((None, tm, dkv), lambda bi, i: (bi, i, P_AV // dkv)),
            pl.BlockSpec((tm, HEAD_DIM), lambda bi, i: (i, 0)),
            pl.BlockSpec((tm, HEAD_DIM), lambda bi, i: (i, 0)),
            pl.BlockSpec((1, HEAD_DIM), lambda bi, i: (0, 0)),
            pl.BlockSpec((1, HEAD_DIM), lambda bi, i: (0, 0)),
        ],
        out_specs=[
            pl.BlockSpec((None, tm, dq), lambda bi, i: (bi, i, 0)),
            pl.BlockSpec((None, tm, dkv), lambda bi, i: (bi, i, 0)),
            pl.BlockSpec((None, tm, 2 * dkv), lambda bi, i: (bi, i, 0)),
        ],
        out_shape=[
            jax.ShapeDtypeStruct((b, s, dq), BF16),
            jax.ShapeDtypeStruct((b, s, dkv), BF16),
            jax.ShapeDtypeStruct((b, s, 2 * dkv), BF16),
        ],
        compiler_params=_params("parallel", "parallel"),
        name="qkv",
    )(p, p, p, cos, sin, q_norm_g.reshape(1, HEAD_DIM), k_norm_g.reshape(1, HEAD_DIM))


def _attn_scores(qs_ref, k_ref, start, size):
    return _dot_nt(qs_ref[...], k_ref[pl.ds(start, size), :])


def _attn_consume(s, v, m_ref, acc_ref):
    reps = s.shape[1] // LANES
    m_prev = m_ref[...]
    m_new = jnp.maximum(m_prev, jnp.max(s, axis=-1, keepdims=True))
    p = jnp.exp2(s - jnp.concatenate([m_new] * reps, axis=1))
    alpha = jnp.exp2(m_prev - m_new)
    acc_ref[...] = jnp.concatenate([alpha] * 2, axis=1) * acc_ref[...] + _dot(p.astype(BF16), v)
    m_ref[...] = m_new


def _attn_consume_fixed(s, v, shift, acc_ref):
    acc_ref[...] += _dot(jnp.exp2(s - shift).astype(BF16), v)


def _attn_all_keys(consume, qs_ref, k_ref, v_ref, m_ref, acc_ref, s_ref, ctx_len, ck):
    n_main = (k_ref.shape[0] - ctx_len) // ck
    rem = k_ref.shape[0] - ctx_len - n_main * ck
    chunks = [(0, ctx_len)] + [(ctx_len + m * ck, ck) for m in range(n_main)]
    if rem:
        chunks.append((ctx_len + n_main * ck, rem))

    def scores(c):
        return _attn_scores(qs_ref, k_ref, chunks[c][0], chunks[c][1])

    def values(c):
        return v_ref[chunks[c][0]:chunks[c][0] + chunks[c][1], :]

    def main_start(m):
        return pl.multiple_of(ctx_len + m * ck, LANES)

    first = scores(0)
    s_ref[...] = scores(1)
    consume(first, values(0), m_ref, acc_ref)

    def body(i, carry):
        s_odd = _attn_scores(qs_ref, k_ref, main_start(2 * i + 1), ck)
        consume(s_ref[...], v_ref[pl.ds(main_start(2 * i), ck), :], m_ref, acc_ref)
        s_ref[...] = _attn_scores(qs_ref, k_ref, main_start(2 * i + 2), ck)
        consume(s_odd, v_ref[pl.ds(main_start(2 * i + 1), ck), :], m_ref, acc_ref)
        return carry

    pairs = (n_main - 1) // 2
    lax.fori_loop(0, pairs, body, 0)
    pending = s_ref[...]
    for c in range(1 + 2 * pairs, len(chunks)):
        upcoming = scores(c + 1) if c + 1 < len(chunks) else None
        consume(pending, values(c), m_ref, acc_ref)
        pending = upcoming


def _attn_kernel(qb_ref, q_ref, k_ref, v_ref, o_ref, qs_ref, m_ref, acc_ref, s_ref, shift_ref, *, ctx_len, ck):
    tq = q_ref.shape[0]
    is_ctx_tile = pl.program_id(2) == 0
    for h in range(ATT_GROUP):
        qs_ref[h * tq:(h + 1) * tq, :] = q_ref[:, h * HEAD_DIM:(h + 1) * HEAD_DIM]
    acc_ref[...] = jnp.zeros_like(acc_ref)

    @pl.when(is_ctx_tile)
    def _():
        kf = k_ref[...].astype(F32)
        shift_ref[0] = qb_ref[0, 0] * jnp.max(jnp.sqrt(jnp.sum(kf * kf, axis=-1, keepdims=True)))
        m_ref[...] = jnp.full_like(m_ref, -jnp.inf)
        _attn_consume(_attn_scores(qs_ref, k_ref, 0, ctx_len), v_ref[0:ctx_len, :], m_ref, acc_ref)

    @pl.when(jnp.logical_not(is_ctx_tile))
    def _():
        shift = shift_ref[0]
        bound_is_safe = shift <= ATTN_SAFE_SHIFT

        @pl.when(bound_is_safe)
        def _():
            _attn_all_keys(_attn_consume_fixed, qs_ref, k_ref, v_ref, shift, acc_ref, s_ref, ctx_len, ck)

        @pl.when(jnp.logical_not(bound_is_safe))
        def _():
            m_ref[...] = jnp.full_like(m_ref, -jnp.inf)
            _attn_all_keys(_attn_consume, qs_ref, k_ref, v_ref, m_ref, acc_ref, s_ref, ctx_len, ck)

    acc = acc_ref[...]
    out = acc[:, :HEAD_DIM] / acc[:, HEAD_DIM:]
    for h in range(ATT_GROUP):
        o_ref[:, h * HEAD_DIM:(h + 1) * HEAD_DIM] = out[h * tq:(h + 1) * tq].astype(o_ref.dtype)


def _attn(qn, kn, vn, q_bound, ctx_len):
    b, s, dq = qn.shape
    tq = SEQ_BLOCK
    ck = min(ATTN_CHUNK, s - ctx_len)
    gw = ATT_GROUP * HEAD_DIM
    rows = ATT_GROUP * tq
    return pl.pallas_call(
        functools.partial(_attn_kernel, ctx_len=ctx_len, ck=ck),
        grid=(b, ATT_KV_HEADS, s // tq),
        in_specs=[
            pl.BlockSpec(memory_space=pltpu.SMEM),
            pl.BlockSpec((None, tq, gw), lambda bi, h, i: (bi, i, h)),
            pl.BlockSpec((None, s, HEAD_DIM), lambda bi, h, i: (bi, 0, h)),
            pl.BlockSpec((None, s, 2 * HEAD_DIM), lambda bi, h, i: (bi, 0, h)),
        ],
        out_specs=pl.BlockSpec((None, tq, gw), lambda bi, h, i: (bi, i, h)),
        out_shape=jax.ShapeDtypeStruct((b, s, dq), BF16),
        scratch_shapes=[
            pltpu.VMEM((rows, HEAD_DIM), BF16),
            pltpu.VMEM((rows, LANES), F32),
            pltpu.VMEM((rows, 2 * HEAD_DIM), F32),
            pltpu.VMEM((rows, ck), F32),
            pltpu.SMEM((1,), F32),
        ],
        compiler_params=_params("parallel", "parallel", "arbitrary"),
        name="attn",
    )(q_bound, qn, kn, vn)


def _merge_kernel(x_ref, y1_ref, y2_ref, y3_ref, bg_ref, modb_ref, modc_ref, wb_ref, wo_ref, o_ref,
                  *, tm, ctx_len):
    d = x_ref.shape[1]
    gate = jax.nn.sigmoid(bg_ref[...].astype(F32))
    m = (gate[:, 0:d] * _dot(y1_ref[...], wb_ref[0])
         + gate[:, d:2 * d] * _dot(y2_ref[...], wb_ref[1])
         + gate[:, 2 * d:3 * d] * _dot(y3_ref[...], wb_ref[2]))
    upd = _dot(m.astype(BF16), wo_ref[...])
    is_ctx = _ctx_rows(pl.program_id(1), tm, ctx_len)
    ga = jnp.where(is_ctx, modc_ref[2:3, :], modb_ref[2:3, :])
    o_ref[...] = x_ref[...] + ga * upd


def _merge(xs, y1, y2, y3, p, mods, wb, wo, ctx_len):
    b, s, d = xs.shape
    tm = _first_divisor(s, (768, 640, 256))
    nb = mods.shape[0] - 1
    row = lambda bi, i: (bi, i, 0)
    return pl.pallas_call(
        functools.partial(_merge_kernel, tm=tm, ctx_len=ctx_len),
        grid=(b, s // tm),
        in_specs=[
            pl.BlockSpec((None, tm, d), row),
            pl.BlockSpec((None, tm, d), row),
            pl.BlockSpec((None, tm, d), row),
            pl.BlockSpec((None, tm, d), row),
            pl.BlockSpec((None, tm, N_BRANCHES * d), lambda bi, i: (bi, i, P_BG // (N_BRANCHES * d))),
            pl.BlockSpec((None, 6, d), lambda bi, i: (bi, 0, 0)),
            pl.BlockSpec((None, 6, d), lambda bi, i: (nb, 0, 0)),
            pl.BlockSpec((N_BRANCHES, d, d), lambda bi, i: (0, 0, 0)),
            pl.BlockSpec((d, d), lambda bi, i: (0, 0)),
        ],
        out_specs=pl.BlockSpec((None, tm, d), row),
        out_shape=jax.ShapeDtypeStruct((b, s, d), F32),
        compiler_params=_params("parallel", "parallel"),
        name="merge",
    )(xs, y1, y2, y3, p, mods, mods, wb, wo)


def _mlp_kernel(x_ref, modb_ref, modc_ref, g_ref, w1_ref, w2_ref, o_ref, h_ref, acc_ref, *, tm, ctx_len):
    i = pl.program_id(1)
    j = pl.program_id(2)
    is_ctx = _ctx_rows(i, tm, ctx_len)

    @pl.when(j == 0)
    def _():
        y = _rms(x_ref[...]) * g_ref[...]
        shift = jnp.where(is_ctx, modc_ref[3:4, :], modb_ref[3:4, :])
        scale = jnp.where(is_ctx, modc_ref[4:5, :], modb_ref[4:5, :])
        h_ref[...] = (y * (1.0 + scale) + shift).astype(BF16)
        acc_ref[...] = jnp.zeros_like(acc_ref)

    t = jnp.maximum(_dot(h_ref[...], w1_ref[...]), 0.0)
    acc_ref[...] += _dot((t * t).astype(BF16), w2_ref[...])

    @pl.when(j == pl.num_programs(2) - 1)
    def _():
        ga = jnp.where(is_ctx, modc_ref[5:6, :], modb_ref[5:6, :])
        o_ref[...] = x_ref[...] + ga * acc_ref[...]


def _mlp(xs, mods, norm_g, w1, w2, ctx_len):
    b, s, d = xs.shape
    dff = w1.shape[1]
    tm = _first_divisor(s, (1408, 1280, 640, 256))
    tf = _first_divisor(dff, (1024, 512))
    nb = mods.shape[0] - 1
    return pl.pallas_call(
        functools.partial(_mlp_kernel, tm=tm, ctx_len=ctx_len),
        grid=(b, s // tm, dff // tf),
        in_specs=[
            pl.BlockSpec((None, tm, d), lambda bi, i, j: (bi, i, 0)),
            pl.BlockSpec((None, 6, d), lambda bi, i, j: (bi, 0, 0)),
            pl.BlockSpec((None, 6, d), lambda bi, i, j: (nb, 0, 0)),
            pl.BlockSpec((1, d), lambda bi, i, j: (0, 0)),
            pl.BlockSpec((d, tf), lambda bi, i, j: (0, j)),
            pl.BlockSpec((tf, d), lambda bi, i, j: (j, 0)),
        ],
        out_specs=pl.BlockSpec((None, tm, d), lambda bi, i, j: (bi, i, 0)),
        out_shape=jax.ShapeDtypeStruct((b, s, d), F32),
        scratch_shapes=[pltpu.VMEM((tm, d), BF16), pltpu.VMEM((tm, d), F32)],
        compiler_params=_params("parallel", "parallel", "arbitrary"),
        name="mlp",
    )(xs, mods, mods, norm_g.reshape(1, d), w1, w2)


def _pack_w_in(w_in):
    lx, ly, gq, gk, gv, gr, gz, aq, ak, av, bg = jnp.split(
        w_in.astype(BF16), (1024, 2048, 2560, 3072, 4096, 5120, 5152, 6176, 6432, 6688), axis=-1)
    pad = jnp.zeros(w_in.shape[:-1] + (P_GZ_PAD - gz.shape[-1],), BF16)

    def permute_heads(w):
        q4 = w.reshape(w.shape[:-1] + (w.shape[-1] // HEAD_DIM, 4, HEAD_DIM // 4))
        return jnp.stack([q4[..., 0, :], q4[..., 2, :], q4[..., 1, :], q4[..., 3, :]], axis=-2).reshape(w.shape)

    return jnp.concatenate([lx, ly, gv, gr, permute_heads(aq), gq, gk, bg, permute_heads(ak), av, gz, pad],
                           axis=-1)


def _block_diag(w):
    per = MXU_DIM // LRU_BLOCK_W
    lead = w.shape[:-3]
    wg = w.reshape(lead + (w.shape[-3] // per, per, LRU_BLOCK_W, LRU_BLOCK_W))
    eye = jnp.eye(per, dtype=w.dtype)
    bd = jnp.einsum("...gaij,ac->...gaicj", wg, eye)
    return bd.reshape(lead + (w.shape[-3] // per, MXU_DIM, MXU_DIM)).astype(BF16)


def _pad_gate_w(gate_w):
    n_layers, n_dir, rank, dk = gate_w.shape
    out = jnp.zeros((n_layers, n_dir, LANES, dk), gate_w.dtype)
    for d in range(n_dir):
        out = out.at[:, d, d * rank:(d + 1) * rank].set(gate_w[:, d])
    return out.astype(BF16)


def kernel(x, c, ctx, c_ctx, mod_w, mod_b, norm1_g, norm2_g, w_in, conv_w, conv_b, lru_a_w, lru_a_b,
           lru_x_w, lru_x_b, lru_lambda, gla_gate_w, gla_gate_b, gla_norm_g, q_norm_g, k_norm_g,
           w_branch, w_out, mlp_w1, mlp_w2):
    n_lat = x.shape[1]
    ctx_len = ctx.shape[1]
    assert x.shape[2] == D and ctx_len == SEQ_BLOCK and n_lat % SEQ_BLOCK == 0
    n_layers = mod_w.shape[0]

    xs = jnp.concatenate([ctx, x], axis=1)
    mods = _modulation(c, c_ctx, mod_w, mod_b)
    cos, sin = _rope_tables(n_lat, ctx_len)
    perm = _head_perm()
    wp = _pack_w_in(w_in)
    wa = _block_diag(lru_a_w)
    wx = _block_diag(lru_x_w)
    wg = _pad_gate_w(gla_gate_w)
    wb = w_branch.astype(BF16)
    wo = w_out.astype(BF16)
    w1 = mlp_w1.astype(BF16)
    w2 = mlp_w2.astype(BF16)

    for l in range(n_layers):
        p = _proj(xs, mods[l], norm1_g[l], wp[l], ctx_len)
        lru = lambda dr, **kw: _lru(p, conv_w[l], conv_b[l], wa[l, dr], lru_a_b[l, dr], wx[l, dr],
                                    lru_x_b[l, dr], lru_lambda[l, dr], rev=bool(dr), **kw)
        y1 = lru(1, hf=lru(0))
        gla = lambda dr, **kw: _gla(p, wg[l, dr], gla_gate_b[l, dr], rev=bool(dr), **kw)
        y2 = gla(1, of=gla(0), norm_g=gla_norm_g[l])
        qn, kn, vn = _qkv(p, cos, sin, q_norm_g[l][perm], k_norm_g[l][perm])
        q_bound = Q_SCALE_LOG2 * HEAD_DIM ** 0.5 * (1.0 + 2.0 ** -7) * jnp.max(jnp.abs(q_norm_g[l]))
        y3 = _attn(qn, kn, vn, q_bound.reshape(1, 1), ctx_len)
        x1 = _merge(xs, y1, y2, y3, p, mods[l], wb[l], wo[l], ctx_len)
        xs = _mlp(x1, mods[l], norm2_g[l], w1[l], w2[l], ctx_len)
    return xs[:, ctx_len:]
```
